```python
import math
import jax, jax.numpy as jnp
from jax import lax
import numpy as np

D_MODEL = 1024
BATCH = 16
SEQ = 2048
DEPTH = 1

CTX_LEN = 256
GRID_W = 64
EPS = 1e-6
ROPE_THETA = 10000.0

DA_HEADS = 4
DA_HD = 64
DA_WIDTH = DA_HEADS * 2 * DA_HD
DA_Q_BLOCK = 128

ML_HEADS = 4
ML_HD = 128
ML_WIDTH = ML_HEADS * ML_HD
ML_CONV = 3
ML_CHUNK = 128
ML_GATES = 4 * ML_HEADS

IN_WIDTHS = (DA_WIDTH, DA_WIDTH, DA_WIDTH, DA_WIDTH,
             ML_WIDTH, ML_WIDTH, ML_WIDTH, ML_WIDTH,
             ML_GATES,
             D_MODEL, D_MODEL)
IN_WIDTH = 4 * DA_WIDTH + 4 * ML_WIDTH + ML_GATES + 2 * D_MODEL

kernel_name = "hybrid_diffattn_mlstm_dit_block"


def rms_norm(x, g):
    xf = x.astype(jnp.float32)
    y = xf * lax.rsqrt(jnp.mean(xf * xf, axis=-1, keepdims=True) + EPS)
    return (y * g.astype(jnp.float32)).astype(x.dtype)


def split_cols(p):
    idx = []
    acc = 0
    for w in IN_WIDTHS[:-1]:
        acc += w
        idx.append(acc)
    return jnp.split(p, idx, axis=-1)


def split_heads(a, h):
    b, t, _ = a.shape
    return a.reshape(b, t, h, -1).transpose(0, 2, 1, 3)


def merge_heads(a):
    b, h, t, d = a.shape
    return a.transpose(0, 2, 1, 3).reshape(b, t, h * d)


def axial_rope_angles(n_tokens):
    rows = n_tokens // GRID_W
    row_id = jnp.repeat(jnp.arange(rows, dtype=jnp.float32), GRID_W)
    col_id = jnp.tile(jnp.arange(GRID_W, dtype=jnp.float32), rows)
    n_freq = DA_HD // 4
    inv_freq = ROPE_THETA ** (-jnp.arange(n_freq, dtype=jnp.float32) / n_freq)
    return row_id[:, None] * inv_freq, col_id[:, None] * inv_freq


def rope_1d(x, ang):
    nf = ang.shape[-1]
    x1, x2 = x[..., :nf], x[..., nf:]
    cos, sin = jnp.cos(ang), jnp.sin(ang)
    return jnp.concatenate([x1 * cos - x2 * sin, x1 * sin + x2 * cos], axis=-1)


def rope_2d(x, ang_r, ang_c):
    xf = x.astype(jnp.float32)
    half = x.shape[-1] // 2
    y = jnp.concatenate([rope_1d(xf[..., :half], ang_r), rope_1d(xf[..., half:], ang_c)], axis=-1)
    return y.astype(x.dtype)


def diff_maps(a, g):
    b, t, _ = a.shape
    a = a.reshape(b, t, DA_HEADS, 2, DA_HD).transpose(3, 0, 2, 1, 4)
    a = rms_norm(a, g)
    return a[0], a[1]


def diff_attend(q1, q2, k1, k2, v, lam):
    scale = DA_HD ** -0.5
    s1 = jnp.einsum('bhqd,bhkd->bhqk', q1, k1, preferred_element_type=jnp.float32) * scale
    s2 = jnp.einsum('bhqd,bhkd->bhqk', q2, k2, preferred_element_type=jnp.float32) * scale
    p = jax.nn.softmax(s1, axis=-1) - lam * jax.nn.softmax(s2, axis=-1)
    return jnp.einsum('bhqk,bhkd->bhqd', p.astype(v.dtype), v)


def diff_attention_branch(qa_c, ka_c, va_c, za_c, qa_l, ka_l, va_l, za_l,
                          q_g, k_g, lam, lam_init, head_g, ang_r, ang_c, need_ctx):
    q1c, q2c = diff_maps(qa_c, q_g)
    k1c, k2c = diff_maps(ka_c, k_g)
    vc = split_heads(va_c, DA_HEADS)
    q1l, q2l = diff_maps(qa_l, q_g)
    k1l, k2l = diff_maps(ka_l, k_g)
    q1l, q2l = rope_2d(q1l, ang_r, ang_c), rope_2d(q2l, ang_r, ang_c)
    k1l, k2l = rope_2d(k1l, ang_r, ang_c), rope_2d(k2l, ang_r, ang_c)
    vl = split_heads(va_l, DA_HEADS)
    k1 = jnp.concatenate([k1c, k1l], axis=2)
    k2 = jnp.concatenate([k2c, k2l], axis=2)
    v = jnp.concatenate([vc, vl], axis=2)
    b, h, n, _ = q1l.shape
    nb = n // DA_Q_BLOCK

    def to_blocks(q):
        return q.reshape(b, h, nb, DA_Q_BLOCK, DA_HD).transpose(2, 0, 1, 3, 4)

    out = lax.map(lambda qs: diff_attend(qs[0], qs[1], k1, k2, v, lam),
                  (to_blocks(q1l), to_blocks(q2l)))
    out = out.transpose(1, 2, 0, 3, 4).reshape(b, h, n, 2 * DA_HD)

    def finish(o, z):
        return merge_heads(rms_norm(o, head_g) * (1.0 - lam_init)) * jax.nn.silu(z)

    y_l = finish(out, za_l)
    y_c = finish(diff_attend(q1c, q2c, k1c, k2c, vc, lam), za_c) if need_ctx else None
    return y_l, y_c


def short_conv(x, w, bias):
    k_w = w.shape[0]
    pad = k_w // 2
    t = x.shape[1]
    xp = jnp.pad(x, ((0, 0), (pad, pad), (0, 0)))
    y = bias + w[0] * xp[:, 0:t]
    for k in range(1, k_w):
        y = y + w[k] * xp[:, k:k + t]
    return y


def mlstm_prepare(xm, vm, gates, conv_w, conv_b, wq, wk, if_bias):
    b, t, _ = xm.shape
    xc = jax.nn.silu(short_conv(xm, conv_w, conv_b))
    xh = split_heads(xc, ML_HEADS)
    q = jnp.einsum('bhtd,hde->bhte', xh, wq).astype(jnp.float32)
    k = (jnp.einsum('bhtd,hde->bhte', xh, wk) * (ML_HD ** -0.5)).astype(jnp.float32)
    v = split_heads(vm, ML_HEADS).astype(jnp.float32)
    g = (gates + if_bias).astype(jnp.float32).reshape(b, t, 2, 2, ML_HEADS)
    g = g.transpose(2, 3, 0, 4, 1)
    ig = g[:, 0]
    lf = jax.nn.log_sigmoid(g[:, 1])
    return xc, q, k, v, ig, lf


def mlstm_chunkwise(q, k, v, ig, lf, state):
    b, h, t, d = q.shape
    nc = t // ML_CHUNK
    tril = jnp.tril(jnp.ones((ML_CHUNK, ML_CHUNK), dtype=bool))

    def chunks(a):
        a = a.reshape((b, h, nc, ML_CHUNK) + a.shape[3:])
        return jnp.moveaxis(a, 2, 0)

    def step(carry, inp):
        c_st, n_st, m_st = carry
        qc, kc, vc, ic, fc = inp
        cum_f = jnp.cumsum(fc, axis=-1)
        log_d = cum_f[..., :, None] - cum_f[..., None, :] + ic[..., None, :]
        log_d = jnp.where(tril, log_d, -jnp.inf)
        m_inter = cum_f + m_st[..., None]
        m_t = jnp.maximum(m_inter, jnp.max(log_d, axis=-1))
        s = jnp.einsum('bhtd,bhsd->bhts', qc, kc) * jnp.exp(log_d - m_t[..., None])
        dec = jnp.exp(m_inter - m_t)
        num = jnp.einsum('bhts,bhsd->bhtd', s, vc) + dec[..., None] * jnp.einsum('bhtk,bhkv->bhtv', qc, c_st)
        den = jnp.sum(s, axis=-1) + dec * jnp.einsum('bhtk,bhk->bht', qc, n_st)
        h_out = num / jnp.maximum(jnp.abs(den), jnp.exp(-m_t))[..., None]
        f_tot = cum_f[..., -1]
        g_s = f_tot[..., None] - cum_f + ic
        m_new = jnp.maximum(f_tot + m_st, jnp.max(g_s, axis=-1))
        w_s = jnp.exp(g_s - m_new[..., None])
        carry_dec = jnp.exp(f_tot + m_st - m_new)
        c_new = carry_dec[..., None, None] * c_st + jnp.einsum('bhs,bhsk,bhsv->bhkv', w_s, kc, vc)
        n_new = carry_dec[..., None] * n_st + jnp.einsum('bhs,bhsk->bhk', w_s, kc)
        return (c_new, n_new, m_new), h_out

    state, hs = lax.scan(step, state, (chunks(q), chunks(k), chunks(v), chunks(ig), chunks(lf)))
    hs = jnp.moveaxis(hs, 0, 2).reshape(b, h, t, d)
    return hs, state


def mlstm_bidirectional(qc, kc, vc, igc, lfc, ql, kl, vl, igl, lfl):
    b, h, _, d = ql.shape
    zero = (jnp.zeros((b, h, d, d), jnp.float32), jnp.zeros((b, h, d), jnp.float32),
            jnp.zeros((b, h), jnp.float32))

    def flip(a):
        return jnp.flip(a, axis=2)

    hc_f, st_f = mlstm_chunkwise(qc, kc, vc, igc[0], lfc[0], zero)
    hl_f, _ = mlstm_chunkwise(ql, kl, vl, igl[0], lfl[0], st_f)
    hc_b, st_b = mlstm_chunkwise(flip(qc), flip(kc), flip(vc), flip(igc[1]), flip(lfc[1]), zero)
    hl_b, _ = mlstm_chunkwise(flip(ql), flip(kl), flip(vl), flip(igl[1]), flip(lfl[1]), st_b)
    return hc_f + flip(hc_b), hl_f + flip(hl_b)


def mlstm_branch(xm_c, vm_c, g_c, zb_c, ob_c, xm_l, vm_l, g_l, zb_l, ob_l,
                 conv_w, conv_b, wq, wk, if_bias, head_g, skip, need_ctx):
    xcv_c, qc, kc, vc, igc, lfc = mlstm_prepare(xm_c, vm_c, g_c, conv_w, conv_b, wq, wk, if_bias)
    xcv_l, ql, kl, vl, igl, lfl = mlstm_prepare(xm_l, vm_l, g_l, conv_w, conv_b, wq, wk, if_bias)
    hc, hl = mlstm_bidirectional(qc, kc, vc, igc, lfc, ql, kl, vl, igl, lfl)
    g_heads = head_g.reshape(ML_HEADS, 1, ML_HD)

    def finish(hh, xcv, z, o):
        hh = merge_heads(rms_norm(hh, g_heads)).astype(z.dtype)
        return (jax.nn.sigmoid(o) * hh + skip * xcv) * jax.nn.silu(z)

    y_l = finish(hl, xcv_l, zb_l, ob_l)
    y_c = finish(hc, xcv_c, zb_c, ob_c) if need_ctx else None
    return y_l, y_c


def setup_inputs(seed: int = 0) -> dict:
    key = jax.random.key(seed)
    ks = jax.random.split(key, 32)
    f32 = jnp.float32
    D, L = D_MODEL, DEPTH

    def nrm(k, shape, scale):
        return jax.random.normal(k, shape, f32) * scale

    i_bias = nrm(ks[8], (L, 2, 1, ML_HEADS), 0.1)
    f_bias = jnp.linspace(3.0, 6.0, ML_HEADS, dtype=f32) + nrm(ks[9], (L, 2, 1, ML_HEADS), 0.1)
    b_if = jnp.concatenate([i_bias, f_bias], axis=2).reshape(L, ML_GATES)
    return {
        "x": nrm(ks[0], (BATCH, SEQ, D), 1.0),
        "c": nrm(ks[1], (BATCH, D), 1.0),
        "ctx": nrm(ks[2], (BATCH, CTX_LEN, D), 1.0),
        "c_ctx": nrm(ks[3], (D,), 1.0),
        "norm_w": 1.0 + nrm(ks[4], (L, D), 0.1),
        "w_mod": nrm(ks[5], (L, D, 3 * D), 0.5 * D ** -0.5),
        "b_mod": nrm(ks[6], (L, 3 * D), 0.02),
        "w_in": nrm(ks[7], (L, D, IN_WIDTH), D ** -0.5),
        "b_if": b_if,
        "da_q_norm": 1.0 + nrm(ks[10], (L, DA_HD), 0.1),
        "da_k_norm": 1.0 + nrm(ks[11], (L, DA_HD), 0.1),
        "da_lambda_q1": nrm(ks[12], (L, DA_HD), 0.1),
        "da_lambda_k1": nrm(ks[13], (L, DA_HD), 0.1),
        "da_lambda_q2": nrm(ks[14], (L, DA_HD), 0.1),
        "da_lambda_k2": nrm(ks[15], (L, DA_HD), 0.1),
        "da_head_norm": 1.0 + nrm(ks[16], (L, 2 * DA_HD), 0.1),
        "w_out_a": nrm(ks[17], (L, DA_WIDTH, D), DA_WIDTH ** -0.5),
        "ml_conv_w": nrm(ks[18], (L, ML_CONV, ML_WIDTH), ML_CONV ** -0.5),
        "ml_conv_b": nrm(ks[19], (L, ML_WIDTH), 0.02),
        "ml_wq": nrm(ks[20], (L, ML_HEADS, ML_HD, ML_HD), ML_HD ** -0.5),
        "ml_wk": nrm(ks[21], (L, ML_HEADS, ML_HD, ML_HD), ML_HD ** -0.5),
        "ml_head_norm": 1.0 + nrm(ks[22], (L, ML_WIDTH), 0.1),
        "ml_skip": 1.0 + nrm(ks[23], (L, ML_WIDTH), 0.1),
        "w_out_b": nrm(ks[24], (L, ML_WIDTH, D), ML_WIDTH ** -0.5),
        "w_o": nrm(ks[25], (L, D, D), D ** -0.5),
    }


def reference(x, c, ctx, c_ctx, norm_w, w_mod, b_mod, w_in, b_if, da_q_norm, da_k_norm,
              da_lambda_q1, da_lambda_k1, da_lambda_q2, da_lambda_k2, da_head_norm, w_out_a,
              ml_conv_w, ml_conv_b, ml_wq, ml_wk, ml_head_norm, ml_skip, w_out_b, w_o):
    n_lat = x.shape[1]
    ang_r, ang_c = axial_rope_angles(n_lat)
    for l in range(DEPTH):
        need_ctx = l < DEPTH - 1
        lam_init = 0.8 - 0.6 * math.exp(-0.3 * l)
        mod_l = jax.nn.silu(c) @ w_mod[l] + b_mod[l]
        mod_c = jax.nn.silu(c_ctx) @ w_mod[l] + b_mod[l]
        sh_l, sc_l, gt_l = jnp.split(mod_l, 3, axis=-1)
        sh_c, sc_c, gt_c = jnp.split(mod_c, 3, axis=-1)
        h_l = rms_norm(x, norm_w[l]) * (1.0 + sc_l[:, None]) + sh_l[:, None]
        h_c = rms_norm(ctx, norm_w[l]) * (1.0 + sc_c) + sh_c
        qa_l, ka_l, va_l, za_l, xm_l, vm_l, zb_l, ob_l, gif_l, ga_l, gb_l = split_cols(h_l @ w_in[l])
        qa_c, ka_c, va_c, za_c, xm_c, vm_c, zb_c, ob_c, gif_c, ga_c, gb_c = split_cols(h_c @ w_in[l])
        lam = (jnp.exp(jnp.sum(da_lambda_q1[l].astype(jnp.float32) * da_lambda_k1[l].astype(jnp.float32)))
               - jnp.exp(jnp.sum(da_lambda_q2[l].astype(jnp.float32) * da_lambda_k2[l].astype(jnp.float32)))
               + lam_init)
        ya_l, ya_c = diff_attention_branch(qa_c, ka_c, va_c, za_c, qa_l, ka_l, va_l, za_l,
                                           da_q_norm[l], da_k_norm[l], lam, lam_init,
                                           da_head_norm[l], ang_r, ang_c, need_ctx)
        yb_l, yb_c = mlstm_branch(xm_c, vm_c, gif_c, zb_c, ob_c, xm_l, vm_l, gif_l, zb_l, ob_l,
                                  ml_conv_w[l], ml_conv_b[l], ml_wq[l], ml_wk[l], b_if[l],
                                  ml_head_norm[l], ml_skip[l], need_ctx)
        y_l = (jax.nn.sigmoid(ga_l) * (ya_l @ w_out_a[l])
               + jax.nn.sigmoid(gb_l) * (yb_l @ w_out_b[l])) @ w_o[l]
        if need_ctx:
            y_c = (jax.nn.sigmoid(ga_c) * (ya_c @ w_out_a[l])
                   + jax.nn.sigmoid(gb_c) * (yb_c @ w_out_b[l])) @ w_o[l]
            ctx = ctx + gt_c * y_c
        x = x + gt_l[:, None] * y_l
    return x
```

```python
import functools

import jax
import jax.numpy as jnp
import numpy as np
from jax import lax
from jax.experimental import pallas as pl
from jax.experimental.pallas import tpu as pltpu

F32 = jnp.float32
BF16 = jnp.bfloat16

EPS = 1e-6
ROPE_THETA = 10000.0
GRID_W = 64
LAM_INIT = 0.8 - 0.6 * 1.0

DA_HEADS = 4
DA_HD = 64
DA_HEAD_W = 2 * DA_HD
DA_WIDTH = DA_HEADS * DA_HEAD_W

ML_HEADS = 4
ML_HD = 128
ML_WIDTH = ML_HEADS * ML_HD
ML_GATES = 4 * ML_HEADS
ML_L = 256
GATE_ROWS = 16

LANES = 128
VMEM_LIMIT = 56 * 1024 * 1024

IN_TILE = 512
ATT_TQ = 256


def _sigmoid(x):
    return 1.0 / (1.0 + jnp.exp(-x))


def _silu(x):
    return x * _sigmoid(x)


def _split3(x):
    hi = x.astype(BF16)
    r1 = x - hi.astype(F32)
    mid = r1.astype(BF16)
    lo = (r1 - mid.astype(F32)).astype(BF16)
    return hi, mid, lo


def _mod_kernel(cc_ref, w_ref, b_ref, lq1_ref, lk1_ref, lq2_ref, lk2_ref, mod_ref, lam_ref):
    a = _silu(cc_ref[...]).astype(BF16)
    mod_ref[...] = jnp.dot(a, w_ref[...].astype(BF16), preferred_element_type=F32) + b_ref[...]
    s1 = jnp.sum(lq1_ref[...] * lk1_ref[...], axis=-1, keepdims=True)
    s2 = jnp.sum(lq2_ref[...] * lk2_ref[...], axis=-1, keepdims=True)
    lam = jnp.exp(s1) - jnp.exp(s2) + LAM_INIT
    lam_ref[...] = jnp.broadcast_to(lam, lam_ref.shape)


def _modulation(cc, w_mod, b_mod, lq1, lk1, lq2, lk2):
    rows, d = cc.shape
    n3 = w_mod.shape[1]
    vec = pl.BlockSpec((1, DA_HD), lambda j: (0, 0))
    return pl.pallas_call(
        _mod_kernel,
        grid=(n3 // d,),
        in_specs=[pl.BlockSpec((rows, d), lambda j: (0, 0)),
                  pl.BlockSpec((d, d), lambda j: (0, j)),
                  pl.BlockSpec((1, d), lambda j: (0, j)),
                  vec, vec, vec, vec],
        out_specs=[pl.BlockSpec((rows, d), lambda j: (0, j)),
                   pl.BlockSpec((1, LANES), lambda j: (0, 0))],
        out_shape=[jax.ShapeDtypeStruct((rows, n3), F32),
                   jax.ShapeDtypeStruct((1, LANES), F32)],
        compiler_params=pltpu.CompilerParams(dimension_semantics=("arbitrary",),
                                             vmem_limit_bytes=VMEM_LIMIT),
        name="modulation",
    )(cc, w_mod, b_mod, lq1, lk1, lq2, lk2)


def _group_rms(acc, bd_ref, g_ref):
    ms = jnp.dot((acc * acc).astype(BF16), bd_ref[...], preferred_element_type=F32)
    return acc * lax.rsqrt(ms + EPS) * g_ref[...]


def _rope(x, cos, sin_signed, first_half):
    outs = []
    for c in range(x.shape[1] // LANES):
        xs = x[:, c * LANES:(c + 1) * LANES]
        partner = jnp.where(first_half, pltpu.roll(xs, LANES - 16, 1), pltpu.roll(xs, 16, 1))
        outs.append(xs * cos + partner * sin_signed)
    return jnp.concatenate(outs, axis=1)


def _inproj_kernel(x_ref, sc_ref, sh_ref, nw_ref, w_ref, bif_ref, *rest, latent):
    if latent:
        (bd_ref, gq_ref, gk_ref, cos_ref, sin_ref,
         qa_ref, ka_ref, va_ref, za_ref, xm_ref, vm_ref, zb_ref, ob_ref, gg_ref, gt_ref) = rest
    else:
        bd_ref, gk_ref, ka_ref, va_ref, xm_ref, vm_ref, gt_ref = rest

    xf = x_ref[0]
    ms = jnp.mean(xf * xf, axis=-1, keepdims=True)
    y = xf * lax.rsqrt(ms + EPS) * nw_ref[...]
    h = (y * (1.0 + sc_ref[0]) + sh_ref[0]).astype(BF16)

    col = [0]

    def proj(width):
        c0 = col[0]
        col[0] = c0 + width
        return jnp.dot(h, w_ref[:, c0:c0 + width], preferred_element_type=F32)

    if latent:
        tm = xf.shape[0]
        lane = lax.broadcasted_iota(jnp.int32, (tm, LANES), 1)
        first_half = (lane % 32) < 16
        cos = cos_ref[...]
        sin = sin_ref[...]
        q = _rope(_group_rms(proj(DA_WIDTH), bd_ref, gq_ref), cos, sin, first_half)
        qa_ref[0] = (q * (DA_HD ** -0.5)).astype(BF16)
        k = _rope(_group_rms(proj(DA_WIDTH), bd_ref, gk_ref), cos, sin, first_half)
        ka_ref[0] = k.astype(BF16)
        va_ref[0] = proj(DA_WIDTH).astype(BF16)
        za_ref[0] = _silu(proj(DA_WIDTH)).astype(BF16)
        xm_ref[0] = proj(ML_WIDTH).astype(BF16)
        vm_ref[0] = proj(ML_WIDTH).astype(BF16)
        zb_ref[0] = _silu(proj(ML_WIDTH)).astype(BF16)
        ob_ref[0] = _sigmoid(proj(ML_WIDTH)).astype(BF16)
        for j in range(gg_ref.shape[2] // 512):
            gg_ref[0, :, j * 512:(j + 1) * 512] = _sigmoid(proj(512)).astype(BF16)
    else:
        ka_ref[0] = _group_rms(proj(DA_WIDTH), bd_ref, gk_ref).astype(BF16)
        va_ref[0] = proj(DA_WIDTH).astype(BF16)
        xm_ref[0] = proj(ML_WIDTH).astype(BF16)
        vm_ref[0] = proj(ML_WIDTH).astype(BF16)

    g = proj(LANES).T[:ML_GATES] + bif_ref[...]
    row = lax.broadcasted_iota(jnp.int32, g.shape, 0)
    logsig = jnp.minimum(g, 0.0) - jnp.log(1.0 + jnp.exp(-jnp.abs(g)))
    gt_ref[0] = jnp.where((row // ML_HEADS) % 2 == 1, logsig, g)


def _in_proj(x, sc, sh, norm_w, w, bif, extra, *, latent, per_batch_mod):
    b, t, d = x.shape
    tm = min(IN_TILE, t)
    grid = (b, t // tm)
    const2 = lambda bi, i: (0, 0)
    mod_map = (lambda bi, i: (bi, 0, 0)) if per_batch_mod else (lambda bi, i: (0, 0, 0))
    tok = lambda width, dtype: (pl.BlockSpec((1, tm, width), lambda bi, i: (bi, i, 0)),
                                jax.ShapeDtypeStruct((b, t, width), dtype))
    in_specs = [pl.BlockSpec((1, tm, d), lambda bi, i: (bi, i, 0)),
                pl.BlockSpec((1, 1, d), mod_map),
                pl.BlockSpec((1, 1, d), mod_map),
                pl.BlockSpec((1, d), const2),
                pl.BlockSpec(w.shape, const2),
                pl.BlockSpec(bif.shape, const2)]
    if latent:
        bd, gq, gk, cos, sin = extra
        in_specs += [pl.BlockSpec(bd.shape, const2), pl.BlockSpec(gq.shape, const2),
                     pl.BlockSpec(gk.shape, const2),
                     pl.BlockSpec((tm, LANES), lambda bi, i: (i, 0)),
                     pl.BlockSpec((tm, LANES), lambda bi, i: (i, 0))]
        outs = [tok(DA_WIDTH, BF16)] * 4 + [tok(ML_WIDTH, BF16)] * 4 + [tok(2 * d, BF16)]
    else:
        bd, gk = extra
        in_specs += [pl.BlockSpec(bd.shape, const2), pl.BlockSpec(gk.shape, const2)]
        outs = [tok(DA_WIDTH, BF16)] * 2 + [tok(ML_WIDTH, BF16)] * 2
    outs.append((pl.BlockSpec((1, ML_GATES, tm), lambda bi, i: (bi, 0, i)),
                 jax.ShapeDtypeStruct((b, ML_GATES, t), F32)))
    return pl.pallas_call(
        functools.partial(_inproj_kernel, latent=latent),
        grid=grid,
        in_specs=in_specs,
        out_specs=[o[0] for o in outs],
        out_shape=[o[1] for o in outs],
        compiler_params=pltpu.CompilerParams(dimension_semantics=("parallel", "parallel"),
                                             vmem_limit_bytes=VMEM_LIMIT),
        name="in_proj_latent" if latent else "in_proj_context",
    )(x, sc, sh, norm_w, w, bif, *extra)


def _attn_kernel(q_ref, kc_ref, kl_ref, vc_ref, vl_ref, z_ref, lam_ref, hg_ref, o_ref):
    q = q_ref[0]
    lane = lax.broadcasted_iota(jnp.int32, q.shape, 1)
    zero = jnp.zeros_like(q)
    kc = kc_ref[0]
    kl = kl_ref[0]
    nt = (((1,), (1,)), ((), ()))

    def softmax_parts(qm):
        sc = lax.dot_general(qm, kc, nt, preferred_element_type=F32)
        sl = lax.dot_general(qm, kl, nt, preferred_element_type=F32)
        m = jnp.maximum(jnp.max(sc, axis=-1, keepdims=True), jnp.max(sl, axis=-1, keepdims=True))
        ec = jnp.exp(sc - m)
        el = jnp.exp(sl - m)
        inv = 1.0 / (jnp.sum(ec, axis=-1, keepdims=True) + jnp.sum(el, axis=-1, keepdims=True))
        return ec, el, inv

    e1c, e1l, inv1 = softmax_parts(jnp.where(lane < DA_HD, q, zero))
    e2c, e2l, inv2 = softmax_parts(jnp.where(lane >= DA_HD, q, zero))
    w2 = lam_ref[:, 0:1] * inv2
    pc = (e1c * inv1 - e2c * w2).astype(BF16)
    pl_ = (e1l * inv1 - e2l * w2).astype(BF16)
    o = (jnp.dot(pc, vc_ref[0], preferred_element_type=F32)
         + jnp.dot(pl_, vl_ref[0], preferred_element_type=F32))
    ms = jnp.mean(o * o, axis=-1, keepdims=True)
    on = o * lax.rsqrt(ms + EPS) * hg_ref[...]
    o_ref[0] = (on * (1.0 - LAM_INIT) * z_ref[0].astype(F32)).astype(BF16)


def _diff_attn(qa, ka_c, ka_l, va_c, va_l, za, lam, head_g):
    b, t, _ = qa.shape
    n_ctx = ka_c.shape[1]
    tq = min(ATT_TQ, t)
    qmap = lambda bi, h, i: (bi, i, h)
    kmap = lambda bi, h, i: (bi, 0, h)
    const2 = lambda bi, h, i: (0, 0)
    return pl.pallas_call(
        _attn_kernel,
        grid=(b, DA_HEADS, t // tq),
        in_specs=[pl.BlockSpec((1, tq, DA_HEAD_W), qmap),
                  pl.BlockSpec((1, n_ctx, DA_HEAD_W), kmap),
                  pl.BlockSpec((1, t, DA_HEAD_W), kmap),
                  pl.BlockSpec((1, n_ctx, DA_HEAD_W), kmap),
                  pl.BlockSpec((1, t, DA_HEAD_W), kmap),
                  pl.BlockSpec((1, tq, DA_HEAD_W), qmap),
                  pl.BlockSpec((1, LANES), const2),
                  pl.BlockSpec((1, DA_HEAD_W), const2)],
        out_specs=pl.BlockSpec((1, tq, DA_HEAD_W), qmap),
        out_shape=jax.ShapeDtypeStruct((b, t, DA_WIDTH), BF16),
        compiler_params=pltpu.CompilerParams(
            dimension_semantics=("parallel", "parallel", "arbitrary"),
            vmem_limit_bytes=VMEM_LIMIT),
        name="diff_attn",
    )(qa, ka_c, ka_l, va_c, va_l, za, lam, head_g)


def _conv_silu(x, w_ref, b_ref):
    t = x.shape[0]
    row = lax.broadcasted_iota(jnp.int32, x.shape, 0)
    prev = jnp.where(row == 0, 0.0, pltpu.roll(x, 1, 0))
    nxt = jnp.where(row == t - 1, 0.0, pltpu.roll(x, t - 1, 0))
    y = b_ref[...] + w_ref[0:1, :] * prev
    y = y + w_ref[1:2, :] * x
    y = y + w_ref[2:3, :] * nxt
    return _silu(y)


def _rows_to_cols(rows, eye):
    nt = (((1,), (1,)), ((), ()))
    out = None
    for part in _split3(rows):
        term = lax.dot_general(eye, part, nt, preferred_element_type=F32)
        out = term if out is None else out + term
    return out


def _cumsum_rows(rows, tri):
    out = None
    for part in _split3(rows):
        term = jnp.dot(part, tri, preferred_element_type=F32)
        out = term if out is None else out + term
    return out


def _mlstm_kernel(xl_ref, xc_ref, vl_ref, vc_ref, gl_ref, gc_ref, zb_ref, ob_ref,
                  cw_ref, cb_ref, wq_ref, wkt_ref, hg_ref, sk_ref, o_ref,
                  xcv_s, q_s, kt_s, va_s, hf_s, hb_s, st_s, gate_s, acol_s):
    head = pl.program_id(1)
    n_lat = xl_ref.shape[1]
    nchunk = n_lat // ML_L
    L = ML_L
    nt = (((1,), (1,)), ((), ()))
    wq = wq_ref[0]
    wkt = wkt_ref[0]

    def prepare(x_ref, v_ref, chunk0):
        n = x_ref.shape[1]
        off = chunk0 * L
        xc = _conv_silu(x_ref[0].astype(F32), cw_ref, cb_ref)
        xb = xc.astype(BF16)
        q_s[off:off + n, :] = jnp.dot(xb, wq, preferred_element_type=F32).astype(BF16)
        kt = lax.dot_general(wkt, xb, nt, preferred_element_type=F32) * (ML_HD ** -0.5)
        for j in range(n // L):
            kt_s[chunk0 + j] = kt[:, j * L:(j + 1) * L].astype(BF16)
        va_s[off:off + n, 0:ML_HD] = v_ref[0]
        lane = lax.broadcasted_iota(jnp.int32, (n, ML_HD), 1)
        va_s[off:off + n, ML_HD:2 * ML_HD] = jnp.where(lane == 0, 1.0, 0.0).astype(BF16)
        return xc

    prepare(xc_ref, vc_ref, 0)
    xcv_s[...] = prepare(xl_ref, vl_ref, 1)

    r_i = lax.broadcasted_iota(jnp.int32, (L, L), 0)
    c_i = lax.broadcasted_iota(jnp.int32, (L, L), 1)
    eye = jnp.where(r_i == c_i, 1.0, 0.0).astype(BF16)
    tris = (jnp.where(r_i <= c_i, 1.0, 0.0).astype(BF16), jnp.where(r_i >= c_i, 1.0, 0.0).astype(BF16))
    visible = (c_i <= r_i, c_i >= r_i)

    for d in range(2):
        for g_ref, r0 in ((gl_ref, 0), (gc_ref, nchunk)):
            ig = g_ref[0, (2 * d) * ML_HEADS + head]
            lf = g_ref[0, (2 * d + 1) * ML_HEADS + head]
            n = ig.shape[0]
            if n == 1:
                ig = jnp.broadcast_to(ig, (8, L))
                lf = jnp.broadcast_to(lf, (8, L))
            a = _cumsum_rows(lf, tris[d])
            bb = ig - a
            gate_s[(2 * d) * GATE_ROWS + r0:(2 * d) * GATE_ROWS + r0 + n, :] = a[0:n]
            gate_s[(2 * d + 1) * GATE_ROWS + r0:(2 * d + 1) * GATE_ROWS + r0 + n, :] = bb[0:n]
            if r0 == 0:
                acol_s[d] = _rows_to_cols(a, eye)

    def state_update(d, row, chunk, m_old, first):
        off = pl.multiple_of(chunk * L, L)
        a_row = gate_s[pl.ds((2 * d) * GATE_ROWS + row, 1), :]
        b_row = gate_s[pl.ds((2 * d + 1) * GATE_ROWS + row, 1), :]
        lane = lax.broadcasted_iota(jnp.int32, a_row.shape, 1)
        pick = (lane == L - 1) if d == 0 else (lane == 0)
        f_tot = jnp.sum(jnp.where(pick, a_row, 0.0), axis=-1, keepdims=True)
        g_row = f_tot + b_row
        m_new = jnp.maximum(f_tot + m_old, jnp.max(g_row, axis=-1, keepdims=True))
        w_row = jnp.exp(g_row - m_new)
        kw = (kt_s[chunk].astype(F32) * w_row).astype(BF16)
        upd = jnp.dot(kw, va_s[pl.ds(off, L), :], preferred_element_type=F32)
        if first:
            st_s[d] = upd
        else:
            st_s[d] = jnp.exp(f_tot + m_old - m_new) * st_s[d] + upd
        return m_new

    def chunk_output(d, c, m_old, h_s):
        off = pl.multiple_of((c + 1) * L, L)
        b_row = gate_s[pl.ds((2 * d + 1) * GATE_ROWS + c, 1), :]
        logd = jnp.where(visible[d], b_row, -jnp.inf)
        mm = jnp.maximum(jnp.max(logd, axis=-1, keepdims=True), m_old)
        dmat = jnp.exp(logd - mm)
        qc = q_s[pl.ds(off, L), :]
        s = jnp.dot(qc, kt_s[c + 1], preferred_element_type=F32) * dmat
        r1 = jnp.dot(s.astype(BF16), va_s[pl.ds(off, L), :], preferred_element_type=F32)
        r2 = jnp.dot(qc, st_s[d].astype(BF16), preferred_element_type=F32)
        dec = jnp.exp(m_old - mm)
        num = r1[:, 0:ML_HD] + dec * r2[:, 0:ML_HD]
        den = r1[:, ML_HD:ML_HD + 1] + dec * r2[:, ML_HD:ML_HD + 1]
        lane = lax.broadcasted_iota(jnp.int32, (L, nchunk), 1)
        a_col = jnp.sum(jnp.where(lane == c, acol_s[d], 0.0), axis=-1, keepdims=True)
        floor = jnp.exp(-(a_col + mm))
        h_s[pl.ds(pl.multiple_of(c * L, L), L), :] = num / jnp.maximum(jnp.abs(den), floor)

    zero11 = jnp.zeros((1, 1), F32)
    m_f0 = state_update(0, nchunk, 0, zero11, True)
    m_b0 = state_update(1, nchunk, 0, zero11, True)

    def body(i, ms):
        m_f, m_b = ms
        chunk_output(0, i, m_f, hf_s)
        m_f = state_update(0, i, i + 1, m_f, False)
        j = nchunk - 1 - i
        chunk_output(1, j, m_b, hb_s)
        m_b = state_update(1, j, j + 1, m_b, False)
        return m_f, m_b

    lax.fori_loop(0, nchunk, body, (m_f0, m_b0))

    hh = hf_s[...] + hb_s[...]
    ms = jnp.mean(hh * hh, axis=-1, keepdims=True)
    hn = hh * lax.rsqrt(ms + EPS) * hg_ref[...]
    y = (ob_ref[0].astype(F32) * hn + sk_ref[...] * xcv_s[...]) * zb_ref[0].astype(F32)
    o_ref[0] = y.astype(BF16)


def _mlstm(xm_l, xm_c, vm_l, vm_c, gt_l, gt_c, zb, ob, conv_w, conv_b, wq, wkt, head_g, skip):
    b, t, _ = xm_l.shape
    n_ctx = xm_c.shape[1]
    nchunk = t // ML_L
    assert t % (8 * ML_L) == 0 and n_ctx == ML_L and nchunk < GATE_ROWS
    gl = gt_l.reshape(b, ML_GATES, nchunk, ML_L)
    gc = gt_c.reshape(b, ML_GATES, 1, ML_L)
    tokmap = lambda bi, h: (bi, 0, h)
    gmap = lambda bi, h: (bi, 0, 0, 0)
    hvec = lambda bi, h: (0, h)
    wmap = lambda bi, h: (h, 0, 0)
    nrow = nchunk + 1
    return pl.pallas_call(
        _mlstm_kernel,
        grid=(b, ML_HEADS),
        in_specs=[pl.BlockSpec((1, t, ML_HD), tokmap),
                  pl.BlockSpec((1, n_ctx, ML_HD), tokmap),
                  pl.BlockSpec((1, t, ML_HD), tokmap),
                  pl.BlockSpec((1, n_ctx, ML_HD), tokmap),
                  pl.BlockSpec((1, ML_GATES, nchunk, ML_L), gmap),
                  pl.BlockSpec((1, ML_GATES, 1, ML_L), gmap),
                  pl.BlockSpec((1, t, ML_HD), tokmap),
                  pl.BlockSpec((1, t, ML_HD), tokmap),
                  pl.BlockSpec((conv_w.shape[0], ML_HD), hvec),
                  pl.BlockSpec((1, ML_HD), hvec),
                  pl.BlockSpec((1, ML_HD, ML_HD), wmap),
                  pl.BlockSpec((1, ML_HD, ML_HD), wmap),
                  pl.BlockSpec((1, ML_HD), hvec),
                  pl.BlockSpec((1, ML_HD), hvec)],
        out_specs=pl.BlockSpec((1, t, ML_HD), tokmap),
        out_shape=jax.ShapeDtypeStruct((b, t, ML_WIDTH), BF16),
        scratch_shapes=[pltpu.VMEM((t, ML_HD), F32),
                        pltpu.VMEM((n_ctx + t, ML_HD), BF16),
                        pltpu.VMEM((nrow, ML_HD, ML_L), BF16),
                        pltpu.VMEM((n_ctx + t, 2 * ML_HD), BF16),
                        pltpu.VMEM((t, ML_HD), F32),
                        pltpu.VMEM((t, ML_HD), F32),
                        pltpu.VMEM((2, ML_HD, 2 * ML_HD), F32),
                        pltpu.VMEM((4 * GATE_ROWS, ML_L), F32),
                        pltpu.VMEM((2, ML_L, nchunk), F32)],
        compiler_params=pltpu.CompilerParams(dimension_semantics=("parallel", "arbitrary"),
                                             vmem_limit_bytes=VMEM_LIMIT),
        name="mlstm",
    )(xm_l, xm_c, vm_l, vm_c, gl, gc, zb, ob, conv_w, conv_b, wq, wkt, head_g, skip)


def _outproj_kernel(x_ref, ya_ref, yb_ref, gg_ref, gt_ref, woa_ref, wob_ref, wo_ref, o_ref):
    d = x_ref.shape[2]
    a = jnp.dot(ya_ref[0], woa_ref[...], preferred_element_type=F32)
    b = jnp.dot(yb_ref[0], wob_ref[...], preferred_element_type=F32)
    y = gg_ref[0, :, 0:d].astype(F32) * a + gg_ref[0, :, d:2 * d].astype(F32) * b
    o = jnp.dot(y.astype(BF16), wo_ref[...], preferred_element_type=F32)
    o_ref[0] = x_ref[0] + gt_ref[0] * o


def _out_proj(x, ya, yb, gg, gt, woa, wob, wo):
    b, t, d = x.shape
    tm = min(IN_TILE, t)
    tok = lambda width: pl.BlockSpec((1, tm, width), lambda bi, i: (bi, i, 0))
    const2 = lambda bi, i: (0, 0)
    return pl.pallas_call(
        _outproj_kernel,
        grid=(b, t // tm),
        in_specs=[tok(d), tok(DA_WIDTH), tok(ML_WIDTH), tok(2 * d),
                  pl.BlockSpec((1, 1, d), lambda bi, i: (bi, 0, 0)),
                  pl.BlockSpec(woa.shape, const2), pl.BlockSpec(wob.shape, const2),
                  pl.BlockSpec(wo.shape, const2)],
        out_specs=tok(d),
        out_shape=jax.ShapeDtypeStruct((b, t, d), x.dtype),
        compiler_params=pltpu.CompilerParams(dimension_semantics=("parallel", "parallel"),
                                             vmem_limit_bytes=VMEM_LIMIT),
        name="out_proj",
    )(x, ya, yb, gg, gt, woa, wob, wo)


def _rope_tables(n_tokens):
    rows = n_tokens // GRID_W
    row_id = jnp.repeat(jnp.arange(rows, dtype=F32), GRID_W)
    col_id = jnp.tile(jnp.arange(GRID_W, dtype=F32), rows)
    n_freq = DA_HD // 4
    inv_freq = ROPE_THETA ** (-jnp.arange(n_freq, dtype=F32) / n_freq)
    ang_r = row_id[:, None] * inv_freq
    ang_c = col_id[:, None] * inv_freq
    cos = jnp.concatenate([jnp.cos(ang_r)] * 2 + [jnp.cos(ang_c)] * 2, axis=-1)
    sin = jnp.concatenate([-jnp.sin(ang_r), jnp.sin(ang_r), -jnp.sin(ang_c), jnp.sin(ang_c)], axis=-1)
    return jnp.tile(cos, (1, LANES // DA_HD)), jnp.tile(sin, (1, LANES // DA_HD))


def kernel(x, c, ctx, c_ctx, norm_w, w_mod, b_mod, w_in, b_if, da_q_norm, da_k_norm, da_lambda_q1, da_lambda_k1, da_lambda_q2, da_lambda_k2, da_head_norm, w_out_a, ml_conv_w, ml_conv_b, ml_wq, ml_wk, ml_head_norm, ml_skip, w_out_b, w_o):
    assert w_mod.shape[0] == 1, "single-layer block"
    b, t, d = x.shape

    rows = -(-(b + 1) // 8) * 8
    cc = jnp.zeros((rows, d), F32).at[:b].set(c).at[b].set(c_ctx)
    mod, lam = _modulation(cc, w_mod[0], b_mod, da_lambda_q1, da_lambda_k1, da_lambda_q2, da_lambda_k2)
    sh_l, sc_l, gt_l = (mod[:b, i * d:(i + 1) * d].reshape(b, 1, d) for i in range(3))
    sh_c, sc_c = (mod[b:b + 1, i * d:(i + 1) * d].reshape(1, 1, d) for i in range(2))

    w = w_in[0]
    n_main = 4 * DA_WIDTH + 4 * ML_WIDTH
    w_gates = jnp.pad(w[:, n_main:n_main + ML_GATES], ((0, 0), (0, LANES - ML_GATES)))
    w_lat = jnp.concatenate([w[:, :n_main], w[:, n_main + ML_GATES:], w_gates], axis=1).astype(BF16)
    w_ctx = jnp.concatenate([w[:, DA_WIDTH:3 * DA_WIDTH],
                             w[:, 4 * DA_WIDTH:4 * DA_WIDTH + 2 * ML_WIDTH], w_gates], axis=1).astype(BF16)
    bif = b_if.reshape(ML_GATES, 1)

    grp = np.arange(DA_WIDTH) // DA_HD
    bd = jnp.asarray((grp[:, None] == grp[None, :]).astype(np.float32) / DA_HD, dtype=BF16)
    gq = jnp.tile(da_q_norm, (1, DA_WIDTH // DA_HD))
    gk = jnp.tile(da_k_norm, (1, DA_WIDTH // DA_HD))
    cos, sin = _rope_tables(t)

    qa, ka_l, va_l, za, xm_l, vm_l, zb, ob, gg, gtl = _in_proj(
        x, sc_l, sh_l, norm_w, w_lat, bif, (bd, gq, gk, cos, sin), latent=True, per_batch_mod=True)
    ka_c, va_c, xm_c, vm_c, gtc = _in_proj(
        ctx, sc_c, sh_c, norm_w, w_ctx, bif, (bd, gk), latent=False, per_batch_mod=False)

    ya = _diff_attn(qa, ka_c, ka_l, va_c, va_l, za, lam, da_head_norm)
    yb = _mlstm(xm_l, xm_c, vm_l, vm_c, gtl, gtc, zb, ob, ml_conv_w[0], ml_conv_b,
                ml_wq[0].astype(BF16), jnp.swapaxes(ml_wk[0], 1, 2).astype(BF16), ml_head_norm, ml_skip)
    return _out_proj(x, ya, yb, gg, gt_l, w_out_a[0].astype(BF16), w_out_b[0].astype(BF16),
                     w_o[0].astype(BF16))
```

```python
import functools

import jax
import jax.numpy as jnp
import numpy as np
from jax import lax
from jax.experimental import pallas as pl
from jax.experimental.pallas import tpu as pltpu

F32 = jnp.float32
BF16 = jnp.bfloat16

EPS = 1e-6
ROPE_THETA = 10000.0
GRID_W = 64
LAM_INIT = 0.8 - 0.6 * 1.0

DA_HEADS = 4
DA_HD = 64
DA_HEAD_W = 2 * DA_HD
DA_WIDTH = DA_HEADS * DA_HEAD_W

ML_HEADS = 4
ML_HD = 128
ML_WIDTH = ML_HEADS * ML_HD
ML_GATES = 4 * ML_HEADS
ML_L = 256
GATE_ROWS = 16

LANES = 128
VMEM_LIMIT = 56 * 1024 * 1024

IN_TILE = 512
ATT_BLK = 256
ATT_KC = 256
ATT_UNROLL = 3
ATT_EXP2_HEADROOM = 30.0
SCORE_BOUND_MARGIN = 1.02
Q_SCALE = DA_HD ** -0.5 * 1.4426950408889634


def _sigmoid(x):
    return 1.0 / (1.0 + jnp.exp(-x))


def _silu(x):
    return x * _sigmoid(x)


def _split3(x):
    hi = x.astype(BF16)
    r1 = x - hi.astype(F32)
    mid = r1.astype(BF16)
    lo = (r1 - mid.astype(F32)).astype(BF16)
    return hi, mid, lo


def _mod_kernel(cc_ref, w_ref, b_ref, lq1_ref, lk1_ref, lq2_ref, lk2_ref, gq_ref, gk_ref,
                mod_ref, lam_ref, bound_ref):
    a = _silu(cc_ref[...]).astype(BF16)
    mod_ref[...] = jnp.dot(a, w_ref[...].astype(BF16), preferred_element_type=F32) + b_ref[...]
    s1 = jnp.sum(lq1_ref[...] * lk1_ref[...], axis=-1, keepdims=True)
    s2 = jnp.sum(lq2_ref[...] * lk2_ref[...], axis=-1, keepdims=True)
    lam = jnp.exp(s1) - jnp.exp(s2) + LAM_INIT
    lam_ref[...] = jnp.broadcast_to(lam, lam_ref.shape)
    bq = jnp.max(jnp.abs(gq_ref[...]), axis=-1, keepdims=True)
    bk = jnp.max(jnp.abs(gk_ref[...]), axis=-1, keepdims=True)
    bound_ref[...] = jnp.broadcast_to(SCORE_BOUND_MARGIN * Q_SCALE * DA_HD * bq * bk, bound_ref.shape)


def _modulation(cc, w_mod, b_mod, lq1, lk1, lq2, lk2, gq, gk):
    rows, d = cc.shape
    n3 = w_mod.shape[1]
    vec = pl.BlockSpec((1, DA_HD), lambda j: (0, 0))
    return pl.pallas_call(
        _mod_kernel,
        grid=(n3 // d,),
        in_specs=[pl.BlockSpec((rows, d), lambda j: (0, 0)),
                  pl.BlockSpec((d, d), lambda j: (0, j)),
                  pl.BlockSpec((1, d), lambda j: (0, j)),
                  vec, vec, vec, vec, vec, vec],
        out_specs=[pl.BlockSpec((rows, d), lambda j: (0, j)),
                   pl.BlockSpec((1, LANES), lambda j: (0, 0)),
                   pl.BlockSpec((1, LANES), lambda j: (0, 0))],
        out_shape=[jax.ShapeDtypeStruct((rows, n3), F32),
                   jax.ShapeDtypeStruct((1, LANES), F32),
                   jax.ShapeDtypeStruct((1, LANES), F32)],
        compiler_params=pltpu.CompilerParams(dimension_semantics=("arbitrary",),
                                             vmem_limit_bytes=VMEM_LIMIT),
        name="modulation",
    )(cc, w_mod, b_mod, lq1, lk1, lq2, lk2, gq, gk)


def _group_rms(acc, bd_ref, g_ref):
    ms = jnp.dot((acc * acc).astype(BF16), bd_ref[...], preferred_element_type=F32)
    return acc * lax.rsqrt(ms + EPS) * g_ref[...]


def _rope(x, cos, sin_signed, first_half):
    outs = []
    for c in range(x.shape[1] // LANES):
        xs = x[:, c * LANES:(c + 1) * LANES]
        partner = jnp.where(first_half, pltpu.roll(xs, LANES - 16, 1), pltpu.roll(xs, 16, 1))
        outs.append(xs * cos + partner * sin_signed)
    return jnp.concatenate(outs, axis=1)


def _inproj_kernel(x_ref, sc_ref, sh_ref, nw_ref, w_ref, wvt_ref, bif_ref, *rest, latent):
    if latent:
        (bd_ref, gq_ref, gk_ref, cos_ref, sin_ref,
         qa_ref, ka_ref, vt_ref, za_ref, xm_ref, vm_ref, zb_ref, ob_ref, gg_ref, gt_ref) = rest
    else:
        bd_ref, gk_ref, _, _, ka_ref, vt_ref, xm_ref, vm_ref, gt_ref = rest

    xf = x_ref[0]
    ms = jnp.mean(xf * xf, axis=-1, keepdims=True)
    y = xf * lax.rsqrt(ms + EPS) * nw_ref[...]
    h = (y * (1.0 + sc_ref[0]) + sh_ref[0]).astype(BF16)

    col = [0]

    def proj(width):
        c0 = col[0]
        col[0] = c0 + width
        return jnp.dot(h, w_ref[:, c0:c0 + width], preferred_element_type=F32)

    vt = lax.dot_general(wvt_ref[...], h, (((1,), (1,)), ((), ())), preferred_element_type=F32).astype(BF16)
    for j in range(vt_ref.shape[1]):
        vt_ref[0, j] = vt[:, j * ATT_KC:(j + 1) * ATT_KC]

    if latent:
        tm = xf.shape[0]
        lane = lax.broadcasted_iota(jnp.int32, (tm, LANES), 1)
        first_half = (lane % 32) < 16
        cos = cos_ref[...]
        sin = sin_ref[...]
        q = _rope(_group_rms(proj(DA_WIDTH), bd_ref, gq_ref), cos, sin, first_half)
        qa_ref[0] = (q * Q_SCALE).astype(BF16)
        k = _rope(_group_rms(proj(DA_WIDTH), bd_ref, gk_ref), cos, sin, first_half)
        ka_ref[0] = k.astype(BF16)
        za_ref[0] = _silu(proj(DA_WIDTH)).astype(BF16)
        xm_ref[0] = proj(ML_WIDTH).astype(BF16)
        vm_ref[0] = proj(ML_WIDTH).astype(BF16)
        zb_ref[0] = _silu(proj(ML_WIDTH)).astype(BF16)
        ob_ref[0] = _sigmoid(proj(ML_WIDTH)).astype(BF16)
        for j in range(gg_ref.shape[2] // 512):
            gg_ref[0, :, j * 512:(j + 1) * 512] = _sigmoid(proj(512)).astype(BF16)
    else:
        ka_ref[0] = _group_rms(proj(DA_WIDTH), bd_ref, gk_ref).astype(BF16)
        xm_ref[0] = proj(ML_WIDTH).astype(BF16)
        vm_ref[0] = proj(ML_WIDTH).astype(BF16)

    g = proj(LANES).T[:ML_GATES] + bif_ref[...]
    row = lax.broadcasted_iota(jnp.int32, g.shape, 0)
    logsig = jnp.minimum(g, 0.0) - jnp.log(1.0 + jnp.exp(-jnp.abs(g)))
    gt_ref[0] = jnp.where((row // ML_HEADS) % 2 == 1, logsig, g)


def _in_proj(x, sc, sh, norm_w, w, wvt, bif, extra, *, latent, n_keys, key_off):
    b, t, d = x.shape
    tm = min(IN_TILE, t)
    assert key_off % tm == 0
    kblk = key_off // tm
    grid = (b, t // tm)
    const2 = lambda bi, i: (0, 0)
    mod_map = (lambda bi, i: (bi, 0, 0)) if latent else (lambda bi, i: (0, 0, 0))
    tok = lambda width, dtype: (pl.BlockSpec((1, tm, width), lambda bi, i: (bi, i, 0)),
                                jax.ShapeDtypeStruct((b, t, width), dtype))
    keys = (pl.BlockSpec((1, tm, DA_WIDTH), lambda bi, i: (bi, kblk + i, 0)),
            jax.ShapeDtypeStruct((b, n_keys, DA_WIDTH), BF16))
    cpt = tm // ATT_KC
    vals = (pl.BlockSpec((1, cpt, DA_WIDTH, ATT_KC), lambda bi, i: (bi, kblk + i, 0, 0)),
            jax.ShapeDtypeStruct((b, n_keys // ATT_KC, DA_WIDTH, ATT_KC), BF16))
    in_specs = [pl.BlockSpec((1, tm, d), lambda bi, i: (bi, i, 0)),
                pl.BlockSpec((1, 1, d), mod_map),
                pl.BlockSpec((1, 1, d), mod_map),
                pl.BlockSpec((1, d), const2),
                pl.BlockSpec(w.shape, const2),
                pl.BlockSpec(wvt.shape, const2),
                pl.BlockSpec(bif.shape, const2)]
    if latent:
        bd, gq, gk, cos, sin = extra
        in_specs += [pl.BlockSpec(bd.shape, const2), pl.BlockSpec(gq.shape, const2),
                     pl.BlockSpec(gk.shape, const2),
                     pl.BlockSpec((tm, LANES), lambda bi, i: (i, 0)),
                     pl.BlockSpec((tm, LANES), lambda bi, i: (i, 0))]
        outs = ([tok(DA_WIDTH, BF16), keys, vals, tok(DA_WIDTH, BF16)] + [tok(ML_WIDTH, BF16)] * 4
                + [tok(2 * d, BF16)])
        aliases = {}
    else:
        bd, gk, _, _ = extra
        in_specs += [pl.BlockSpec(bd.shape, const2), pl.BlockSpec(gk.shape, const2),
                     pl.BlockSpec(memory_space=pl.ANY), pl.BlockSpec(memory_space=pl.ANY)]
        outs = [keys, vals] + [tok(ML_WIDTH, BF16)] * 2
        aliases = {len(in_specs) - 2: 0, len(in_specs) - 1: 1}
    outs.append((pl.BlockSpec((1, ML_GATES, tm), lambda bi, i: (bi, 0, i)),
                 jax.ShapeDtypeStruct((b, ML_GATES, t), F32)))
    return pl.pallas_call(
        functools.partial(_inproj_kernel, latent=latent),
        grid=grid,
        in_specs=in_specs,
        out_specs=[o[0] for o in outs],
        out_shape=[o[1] for o in outs],
        input_output_aliases=aliases,
        compiler_params=pltpu.CompilerParams(dimension_semantics=("parallel", "parallel"),
                                             vmem_limit_bytes=VMEM_LIMIT),
        name="in_proj_latent" if latent else "in_proj_context",
    )(x, sc, sh, norm_w, w, wvt, bif, *extra)


def _attn_kernel(bound_ref, q_ref, k_ref, vt_ref, z_ref, lam_ref, hg_ref, o_ref, s_scr, e_scr, acc_scr, qm_scr):
    nblk = q_ref.shape[1] // ATT_BLK
    nkc = k_ref.shape[1] // ATT_KC
    sub = ATT_KC // 8
    lam = lam_ref[:, 0:1]
    nt = (((1,), (1,)), ((), ()))
    neg = jnp.full((8, ATT_BLK), -jnp.inf, F32)
    zero8 = jnp.zeros((8, ATT_BLK), F32)
    bound = bound_ref[0]

    def mask_queries(j):
        q = q_ref[0, j * ATT_BLK:(j + 1) * ATT_BLK, :]
        lane = lax.broadcasted_iota(jnp.int32, q.shape, 1)
        zero = jnp.zeros_like(q)
        qm_scr[0] = jnp.where(lane < DA_HD, q, zero)
        qm_scr[1] = jnp.where(lane >= DA_HD, q, zero)

    def scores(c, mp):
        kc = k_ref[0, pl.ds(pl.multiple_of(c * ATT_KC, ATT_KC), ATT_KC), :]
        return lax.dot_general(kc, qm_scr[mp], nt, preferred_element_type=F32).reshape(sub, 8, ATT_BLK)

    def exps(c, par, mp, st, stab8, lacc):
        e = jnp.exp2(st - stab8[None])
        e_scr[par, mp, c] = e.reshape(ATT_KC, ATT_BLK)
        return lacc + jnp.sum(e, axis=0)

    def values(c, par, r8):
        e1 = e_scr[par, 0, c].reshape(sub, 8, ATT_BLK)
        e2 = e_scr[par, 1, c].reshape(sub, 8, ATT_BLK)
        pt = (e1 - e2 * r8[None]).reshape(ATT_KC, ATT_BLK).astype(BF16)
        acc_scr[...] += jnp.dot(vt_ref[0, c], pt, preferred_element_type=F32)

    def finish(j, inv_l1):
        o = (acc_scr[...] * inv_l1).T
        ms = jnp.mean(o * o, axis=-1, keepdims=True)
        on = o * lax.rsqrt(ms + EPS) * hg_ref[...]
        rows = slice(j * ATT_BLK, (j + 1) * ATT_BLK)
        o_ref[0, rows, :] = (on * (1.0 - LAM_INIT) * z_ref[0, rows, :].astype(F32)).astype(BF16)

    def pipeline(exact_max):
        d_exp = 1 if exact_max else 0
        d_val = d_exp + 1
        bound8 = jnp.full((8, ATT_BLK), bound, F32)
        m8, r8, inv_l1 = {}, {}, {}
        for j in range(nblk + d_val):
            do_s, do_e, do_v = j < nblk, d_exp <= j < nblk + d_exp, d_val <= j
            if do_s:
                mask_queries(j)
            if do_v:
                acc_scr[...] = jnp.zeros_like(acc_scr)

            def body(c, carry, j=j, do_s=do_s, do_e=do_e, do_v=do_v):
                macc, lacc = carry
                if exact_max:
                    if do_s:
                        macc = list(macc)
                        for mp in range(2):
                            st = scores(c, mp)
                            s_scr[j % 2, mp, c] = st.reshape(ATT_KC, ATT_BLK)
                            macc[mp] = jnp.maximum(macc[mp], jnp.max(st, axis=0))
                        macc = tuple(macc)
                    if do_e:
                        jb = j - d_exp
                        lacc = tuple(
                            exps(c, jb % 2, mp, s_scr[jb % 2, mp, c].reshape(sub, 8, ATT_BLK), m8[jb][mp], lacc[mp])
                            for mp in range(2))
                elif do_s:
                    lacc = tuple(exps(c, j % 2, mp, scores(c, mp), bound8, lacc[mp]) for mp in range(2))
                if do_v:
                    values(c, (j - d_val) % 2, r8[j - d_val])
                return macc, lacc

            macc, lacc = lax.fori_loop(0, nkc, body, ((neg, neg), (zero8, zero8)),
                                       unroll=ATT_UNROLL if exact_max else nkc)
            if do_v:
                finish(j - d_val, inv_l1.pop(j - d_val))
                r8.pop(j - d_val)
            if do_e:
                l1 = jnp.sum(lacc[0], axis=0, keepdims=True)
                l2 = jnp.sum(lacc[1], axis=0, keepdims=True)
                r8[j - d_exp] = jnp.broadcast_to(lam * l1 / l2, (8, ATT_BLK))
                inv_l1[j - d_exp] = 1.0 / l1
                m8.pop(j - d_exp, None)
            if exact_max and do_s:
                m8[j] = tuple(jnp.broadcast_to(jnp.max(a, axis=0, keepdims=True), (8, ATT_BLK)) for a in macc)

    use_bound = bound <= 0.5 * (126.0 - ATT_EXP2_HEADROOM)

    @pl.when(use_bound)
    def _():
        pipeline(False)

    @pl.when(jnp.logical_not(use_bound))
    def _():
        pipeline(True)


def _diff_attn(bound, qa, k_all, vt_all, za, lam, head_g):
    b, t, _ = qa.shape
    n_keys = k_all.shape[1]
    nkc = n_keys // ATT_KC
    assert t % ATT_BLK == 0 and n_keys % ATT_KC == 0 and nkc % ATT_UNROLL == 0
    qmap = lambda bi, h: (bi, 0, h)
    const2 = lambda bi, h: (0, 0)
    return pl.pallas_call(
        _attn_kernel,
        grid=(b, DA_HEADS),
        in_specs=[pl.BlockSpec(memory_space=pltpu.SMEM),
                  pl.BlockSpec((1, t, DA_HEAD_W), qmap),
                  pl.BlockSpec((1, n_keys, DA_HEAD_W), qmap),
                  pl.BlockSpec((1, nkc, DA_HEAD_W, ATT_KC), lambda bi, h: (bi, 0, h, 0)),
                  pl.BlockSpec((1, t, DA_HEAD_W), qmap),
                  pl.BlockSpec((1, LANES), const2),
                  pl.BlockSpec((1, DA_HEAD_W), const2)],
        out_specs=pl.BlockSpec((1, t, DA_HEAD_W), qmap),
        out_shape=jax.ShapeDtypeStruct((b, t, DA_WIDTH), BF16),
        scratch_shapes=[pltpu.VMEM((2, 2, nkc, ATT_KC, ATT_BLK), F32),
                        pltpu.VMEM((2, 2, nkc, ATT_KC, ATT_BLK), F32),
                        pltpu.VMEM((DA_HEAD_W, ATT_BLK), F32),
                        pltpu.VMEM((2, ATT_BLK, DA_HEAD_W), BF16)],
        compiler_params=pltpu.CompilerParams(dimension_semantics=("parallel", "arbitrary"),
                                             vmem_limit_bytes=VMEM_LIMIT),
        name="diff_attn",
    )(bound, qa, k_all, vt_all, za, lam, head_g)


def _conv_silu(x, w_ref, b_ref):
    t = x.shape[0]
    row = lax.broadcasted_iota(jnp.int32, x.shape, 0)
    prev = jnp.where(row == 0, 0.0, pltpu.roll(x, 1, 0))
    nxt = jnp.where(row == t - 1, 0.0, pltpu.roll(x, t - 1, 0))
    y = b_ref[...] + w_ref[0:1, :] * prev
    y = y + w_ref[1:2, :] * x
    y = y + w_ref[2:3, :] * nxt
    return _silu(y)


def _rows_to_cols(rows, eye):
    nt = (((1,), (1,)), ((), ()))
    out = None
    for part in _split3(rows):
        term = lax.dot_general(eye, part, nt, preferred_element_type=F32)
        out = term if out is None else out + term
    return out


def _cumsum_rows(rows, tri):
    out = None
    for part in _split3(rows):
        term = jnp.dot(part, tri, preferred_element_type=F32)
        out = term if out is None else out + term
    return out


def _mlstm_kernel(xl_ref, xc_ref, vl_ref, vc_ref, gl_ref, gc_ref, zb_ref, ob_ref,
                  cw_ref, cb_ref, wq_ref, wkt_ref, hg_ref, sk_ref, o_ref,
                  xcv_s, q_s, kt_s, va_s, hf_s, hb_s, st_s, gate_s, acol_s):
    head = pl.program_id(1)
    n_lat = xl_ref.shape[1]
    nchunk = n_lat // ML_L
    L = ML_L
    nt = (((1,), (1,)), ((), ()))
    wq = wq_ref[0]
    wkt = wkt_ref[0]

    def prepare(x_ref, v_ref, chunk0):
        n = x_ref.shape[1]
        off = chunk0 * L
        xc = _conv_silu(x_ref[0].astype(F32), cw_ref, cb_ref)
        xb = xc.astype(BF16)
        q_s[off:off + n, :] = jnp.dot(xb, wq, preferred_element_type=F32).astype(BF16)
        kt = lax.dot_general(wkt, xb, nt, preferred_element_type=F32) * (ML_HD ** -0.5)
        for j in range(n // L):
            kt_s[chunk0 + j] = kt[:, j * L:(j + 1) * L].astype(BF16)
        va_s[off:off + n, 0:ML_HD] = v_ref[0]
        lane = lax.broadcasted_iota(jnp.int32, (n, ML_HD), 1)
        va_s[off:off + n, ML_HD:2 * ML_HD] = jnp.where(lane == 0, 1.0, 0.0).astype(BF16)
        return xc

    prepare(xc_ref, vc_ref, 0)
    xcv_s[...] = prepare(xl_ref, vl_ref, 1)

    r_i = lax.broadcasted_iota(jnp.int32, (L, L), 0)
    c_i = lax.broadcasted_iota(jnp.int32, (L, L), 1)
    eye = jnp.where(r_i == c_i, 1.0, 0.0).astype(BF16)
    tris = (jnp.where(r_i <= c_i, 1.0, 0.0).astype(BF16), jnp.where(r_i >= c_i, 1.0, 0.0).astype(BF16))
    visible = (c_i <= r_i, c_i >= r_i)

    for d in range(2):
        for g_ref, r0 in ((gl_ref, 0), (gc_ref, nchunk)):
            ig = g_ref[0, (2 * d) * ML_HEADS + head]
            lf = g_ref[0, (2 * d + 1) * ML_HEADS + head]
            n = ig.shape[0]
            if n == 1:
                ig = jnp.broadcast_to(ig, (8, L))
                lf = jnp.broadcast_to(lf, (8, L))
            a = _cumsum_rows(lf, tris[d])
            bb = ig - a
            gate_s[(2 * d) * GATE_ROWS + r0:(2 * d) * GATE_ROWS + r0 + n, :] = a[0:n]
            gate_s[(2 * d + 1) * GATE_ROWS + r0:(2 * d + 1) * GATE_ROWS + r0 + n, :] = bb[0:n]
            if r0 == 0:
                acol_s[d] = _rows_to_cols(a, eye)

    def state_update(d, row, chunk, m_old, first):
        off = pl.multiple_of(chunk * L, L)
        a_row = gate_s[pl.ds((2 * d) * GATE_ROWS + row, 1), :]
        b_row = gate_s[pl.ds((2 * d + 1) * GATE_ROWS + row, 1), :]
        lane = lax.broadcasted_iota(jnp.int32, a_row.shape, 1)
        pick = (lane == L - 1) if d == 0 else (lane == 0)
        f_tot = jnp.sum(jnp.where(pick, a_row, 0.0), axis=-1, keepdims=True)
        g_row = f_tot + b_row
        m_new = jnp.maximum(f_tot + m_old, jnp.max(g_row, axis=-1, keepdims=True))
        w_row = jnp.exp(g_row - m_new)
        kw = (kt_s[chunk].astype(F32) * w_row).astype(BF16)
        upd = jnp.dot(kw, va_s[pl.ds(off, L), :], preferred_element_type=F32)
        if first:
            st_s[d] = upd
        else:
            st_s[d] = jnp.exp(f_tot + m_old - m_new) * st_s[d] + upd
        return m_new

    def chunk_output(d, c, m_old, h_s):
        off = pl.multiple_of((c + 1) * L, L)
        b_row = gate_s[pl.ds((2 * d + 1) * GATE_ROWS + c, 1), :]
        logd = jnp.where(visible[d], b_row, -jnp.inf)
        mm = jnp.maximum(jnp.max(logd, axis=-1, keepdims=True), m_old)
        dmat = jnp.exp(logd - mm)
        qc = q_s[pl.ds(off, L), :]
        s = jnp.dot(qc, kt_s[c + 1], preferred_element_type=F32) * dmat
        r1 = jnp.dot(s.astype(BF16), va_s[pl.ds(off, L), :], preferred_element_type=F32)
        r2 = jnp.dot(qc, st_s[d].astype(BF16), preferred_element_type=F32)
        dec = jnp.exp(m_old - mm)
        num = r1[:, 0:ML_HD] + dec * r2[:, 0:ML_HD]
        den = r1[:, ML_HD:ML_HD + 1] + dec * r2[:, ML_HD:ML_HD + 1]
        lane = lax.broadcasted_iota(jnp.int32, (L, nchunk), 1)
        a_col = jnp.sum(jnp.where(lane == c, acol_s[d], 0.0), axis=-1, keepdims=True)
        floor = jnp.exp(-(a_col + mm))
        h_s[pl.ds(pl.multiple_of(c * L, L), L), :] = num / jnp.maximum(jnp.abs(den), floor)

    zero11 = jnp.zeros((1, 1), F32)
    m_f0 = state_update(0, nchunk, 0, zero11, True)
    m_b0 = state_update(1, nchunk, 0, zero11, True)

    def body(i, ms):
        m_f, m_b = ms
        chunk_output(0, i, m_f, hf_s)
        m_f = state_update(0, i, i + 1, m_f, False)
        j = nchunk - 1 - i
        chunk_output(1, j, m_b, hb_s)
        m_b = state_update(1, j, j + 1, m_b, False)
        return m_f, m_b

    lax.fori_loop(0, nchunk, body, (m_f0, m_b0))

    hh = hf_s[...] + hb_s[...]
    ms = jnp.mean(hh * hh, axis=-1, keepdims=True)
    hn = hh * lax.rsqrt(ms + EPS) * hg_ref[...]
    y = (ob_ref[0].astype(F32) * hn + sk_ref[...] * xcv_s[...]) * zb_ref[0].astype(F32)
    o_ref[0] = y.astype(BF16)


def _mlstm(xm_l, xm_c, vm_l, vm_c, gt_l, gt_c, zb, ob, conv_w, conv_b, wq, wkt, head_g, skip):
    b, t, _ = xm_l.shape
    n_ctx = xm_c.shape[1]
    nchunk = t // ML_L
    assert t % (8 * ML_L) == 0 and n_ctx == ML_L and nchunk < GATE_ROWS
    gl = gt_l.reshape(b, ML_GATES, nchunk, ML_L)
    gc = gt_c.reshape(b, ML_GATES, 1, ML_L)
    tokmap = lambda bi, h: (bi, 0, h)
    gmap = lambda bi, h: (bi, 0, 0, 0)
    hvec = lambda bi, h: (0, h)
    wmap = lambda bi, h: (h, 0, 0)
    nrow = nchunk + 1
    return pl.pallas_call(
        _mlstm_kernel,
        grid=(b, ML_HEADS),
        in_specs=[pl.BlockSpec((1, t, ML_HD), tokmap),
                  pl.BlockSpec((1, n_ctx, ML_HD), tokmap),
                  pl.BlockSpec((1, t, ML_HD), tokmap),
                  pl.BlockSpec((1, n_ctx, ML_HD), tokmap),
                  pl.BlockSpec((1, ML_GATES, nchunk, ML_L), gmap),
                  pl.BlockSpec((1, ML_GATES, 1, ML_L), gmap),
                  pl.BlockSpec((1, t, ML_HD), tokmap),
                  pl.BlockSpec((1, t, ML_HD), tokmap),
                  pl.BlockSpec((conv_w.shape[0], ML_HD), hvec),
                  pl.BlockSpec((1, ML_HD), hvec),
                  pl.BlockSpec((1, ML_HD, ML_HD), wmap),
                  pl.BlockSpec((1, ML_HD, ML_HD), wmap),
                  pl.BlockSpec((1, ML_HD), hvec),
                  pl.BlockSpec((1, ML_HD), hvec)],
        out_specs=pl.BlockSpec((1, t, ML_HD), tokmap),
        out_shape=jax.ShapeDtypeStruct((b, t, ML_WIDTH), BF16),
        scratch_shapes=[pltpu.VMEM((t, ML_HD), F32),
                        pltpu.VMEM((n_ctx + t, ML_HD), BF16),
                        pltpu.VMEM((nrow, ML_HD, ML_L), BF16),
                        pltpu.VMEM((n_ctx + t, 2 * ML_HD), BF16),
                        pltpu.VMEM((t, ML_HD), F32),
                        pltpu.VMEM((t, ML_HD), F32),
                        pltpu.VMEM((2, ML_HD, 2 * ML_HD), F32),
                        pltpu.VMEM((4 * GATE_ROWS, ML_L), F32),
                        pltpu.VMEM((2, ML_L, nchunk), F32)],
        compiler_params=pltpu.CompilerParams(dimension_semantics=("parallel", "arbitrary"),
                                             vmem_limit_bytes=VMEM_LIMIT),
        name="mlstm",
    )(xm_l, xm_c, vm_l, vm_c, gl, gc, zb, ob, conv_w, conv_b, wq, wkt, head_g, skip)


def _outproj_kernel(x_ref, ya_ref, yb_ref, gg_ref, gt_ref, woa_ref, wob_ref, wo_ref, o_ref):
    d = x_ref.shape[2]
    a = jnp.dot(ya_ref[0], woa_ref[...], preferred_element_type=F32)
    b = jnp.dot(yb_ref[0], wob_ref[...], preferred_element_type=F32)
    y = gg_ref[0, :, 0:d].astype(F32) * a + gg_ref[0, :, d:2 * d].astype(F32) * b
    o = jnp.dot(y.astype(BF16), wo_ref[...], preferred_element_type=F32)
    o_ref[0] = x_ref[0] + gt_ref[0] * o


def _out_proj(x, ya, yb, gg, gt, woa, wob, wo):
    b, t, d = x.shape
    tm = min(IN_TILE, t)
    tok = lambda width: pl.BlockSpec((1, tm, width), lambda bi, i: (bi, i, 0))
    const2 = lambda bi, i: (0, 0)
    return pl.pallas_call(
        _outproj_kernel,
        grid=(b, t // tm),
        in_specs=[tok(d), tok(DA_WIDTH), tok(ML_WIDTH), tok(2 * d),
                  pl.BlockSpec((1, 1, d), lambda bi, i: (bi, 0, 0)),
                  pl.BlockSpec(woa.shape, const2), pl.BlockSpec(wob.shape, const2),
                  pl.BlockSpec(wo.shape, const2)],
        out_specs=tok(d),
        out_shape=jax.ShapeDtypeStruct((b, t, d), x.dtype),
        compiler_params=pltpu.CompilerParams(dimension_semantics=("parallel", "parallel"),
                                             vmem_limit_bytes=VMEM_LIMIT),
        name="out_proj",
    )(x, ya, yb, gg, gt, woa, wob, wo)


def _rope_tables(n_tokens):
    rows = n_tokens // GRID_W
    row_id = jnp.repeat(jnp.arange(rows, dtype=F32), GRID_W)
    col_id = jnp.tile(jnp.arange(GRID_W, dtype=F32), rows)
    n_freq = DA_HD // 4
    inv_freq = ROPE_THETA ** (-jnp.arange(n_freq, dtype=F32) / n_freq)
    ang_r = row_id[:, None] * inv_freq
    ang_c = col_id[:, None] * inv_freq
    cos = jnp.concatenate([jnp.cos(ang_r)] * 2 + [jnp.cos(ang_c)] * 2, axis=-1)
    sin = jnp.concatenate([-jnp.sin(ang_r), jnp.sin(ang_r), -jnp.sin(ang_c), jnp.sin(ang_c)], axis=-1)
    return jnp.tile(cos, (1, LANES // DA_HD)), jnp.tile(sin, (1, LANES // DA_HD))


def kernel(x, c, ctx, c_ctx, norm_w, w_mod, b_mod, w_in, b_if, da_q_norm, da_k_norm, da_lambda_q1, da_lambda_k1, da_lambda_q2, da_lambda_k2, da_head_norm, w_out_a, ml_conv_w, ml_conv_b, ml_wq, ml_wk, ml_head_norm, ml_skip, w_out_b, w_o):
    assert w_mod.shape[0] == 1, "single-layer block"
    b, t, d = x.shape

    rows = -(-(b + 1) // 8) * 8
    cc = jnp.zeros((rows, d), F32).at[:b].set(c).at[b].set(c_ctx)
    mod, lam, bound = _modulation(cc, w_mod[0], b_mod, da_lambda_q1, da_lambda_k1, da_lambda_q2, da_lambda_k2,
                                  da_q_norm, da_k_norm)
    sh_l, sc_l, gt_l = (mod[:b, i * d:(i + 1) * d].reshape(b, 1, d) for i in range(3))
    sh_c, sc_c = (mod[b:b + 1, i * d:(i + 1) * d].reshape(1, 1, d) for i in range(2))

    w = w_in[0]
    n_main = 4 * DA_WIDTH + 4 * ML_WIDTH
    w_gates = jnp.pad(w[:, n_main:n_main + ML_GATES], ((0, 0), (0, LANES - ML_GATES)))
    w_lat = jnp.concatenate([w[:, :2 * DA_WIDTH], w[:, 3 * DA_WIDTH:n_main], w[:, n_main + ML_GATES:], w_gates],
                            axis=1).astype(BF16)
    w_ctx = jnp.concatenate([w[:, DA_WIDTH:2 * DA_WIDTH],
                             w[:, 4 * DA_WIDTH:4 * DA_WIDTH + 2 * ML_WIDTH], w_gates], axis=1).astype(BF16)
    w_vt = w[:, 2 * DA_WIDTH:3 * DA_WIDTH].T.astype(BF16)
    bif = b_if.reshape(ML_GATES, 1)

    grp = np.arange(DA_WIDTH) // DA_HD
    bd = jnp.asarray((grp[:, None] == grp[None, :]).astype(np.float32) / DA_HD, dtype=BF16)
    gq = jnp.tile(da_q_norm, (1, DA_WIDTH // DA_HD))
    gk = jnp.tile(da_k_norm, (1, DA_WIDTH // DA_HD))
    cos, sin = _rope_tables(t)

    n_keys = t + ctx.shape[1]
    qa, k_all, vt_all, za, xm_l, vm_l, zb, ob, gg, gtl = _in_proj(
        x, sc_l, sh_l, norm_w, w_lat, w_vt, bif, (bd, gq, gk, cos, sin), latent=True, n_keys=n_keys, key_off=0)
    k_all, vt_all, xm_c, vm_c, gtc = _in_proj(
        ctx, sc_c, sh_c, norm_w, w_ctx, w_vt, bif, (bd, gk, k_all, vt_all), latent=False, n_keys=n_keys, key_off=t)

    ya = _diff_attn(bound[0, :1], qa, k_all, vt_all, za, lam, da_head_norm)
    yb = _mlstm(xm_l, xm_c, vm_l, vm_c, gtl, gtc, zb, ob, ml_conv_w[0], ml_conv_b,
                ml_wq[0].astype(BF16), jnp.swapaxes(ml_wk[0], 1, 2).astype(BF16), ml_head_norm, ml_skip)
    return _out_proj(x, ya, yb, gg, gt_l, w_out_a[0].astype(BF16), w_out_b[0].astype(BF16),
                     w_o[0].astype(BF16))
```

```python
import functools

import jax
import jax.numpy as jnp
import numpy as np
from jax import lax
from jax.experimental import pallas as pl
from jax.experimental.pallas import tpu as pltpu

F32 = jnp.float32
BF16 = jnp.bfloat16

EPS = 1e-6
ROPE_THETA = 10000.0
GRID_W = 64
LAM_INIT = 0.8 - 0.6 * 1.0

DA_HEADS = 4
DA_HD = 64
DA_HEAD_W = 2 * DA_HD
DA_WIDTH = DA_HEADS * DA_HEAD_W

ML_HEADS = 4
ML_HD = 128
ML_WIDTH = ML_HEADS * ML_HD
ML_GATES = 4 * ML_HEADS

COL_QA, COL_KA, COL_VA, COL_ZA = (i * DA_WIDTH for i in range(4))
COL_XM, COL_VM, COL_ZB, COL_OB = (4 * DA_WIDTH + i * ML_WIDTH for i in range(4))
N_MAIN = 4 * DA_WIDTH + 4 * ML_WIDTH
ML_L = 256
GATE_ROWS = 16
VA_ROWS = ML_HD + 16
LOG2E = 1.4426950408889634
ML_HPP = 2

LANES = 128
VMEM_LIMIT = 56 * 1024 * 1024

IN_TILE = 512
ATT_BLK = 256
ATT_KC = 256
ATT_UNROLL = 3
ATT_EXP2_HEADROOM = 30.0
SCORE_BOUND_MARGIN = 1.02
Q_SCALE = DA_HD ** -0.5 * LOG2E


def _sigmoid(x):
    return 1.0 / (1.0 + jnp.exp(-x))


def _silu(x):
    return x * _sigmoid(x)


def _split3(x):
    hi = x.astype(BF16)
    r1 = x - hi.astype(F32)
    mid = r1.astype(BF16)
    lo = (r1 - mid.astype(F32)).astype(BF16)
    return hi, mid, lo


def _mod_kernel(cc_ref, w_ref, b_ref, lq1_ref, lk1_ref, lq2_ref, lk2_ref, gq_ref, gk_ref,
                mod_ref, lam_ref, bound_ref):
    a = _silu(cc_ref[...]).astype(BF16)
    mod_ref[...] = jnp.dot(a, w_ref[...].astype(BF16), preferred_element_type=F32) + b_ref[...]
    s1 = jnp.sum(lq1_ref[...] * lk1_ref[...], axis=-1, keepdims=True)
    s2 = jnp.sum(lq2_ref[...] * lk2_ref[...], axis=-1, keepdims=True)
    lam = jnp.exp(s1) - jnp.exp(s2) + LAM_INIT
    lam_ref[...] = jnp.broadcast_to(lam, lam_ref.shape)
    bq = jnp.max(jnp.abs(gq_ref[...]), axis=-1, keepdims=True)
    bk = jnp.max(jnp.abs(gk_ref[...]), axis=-1, keepdims=True)
    bound_ref[...] = jnp.broadcast_to(SCORE_BOUND_MARGIN * Q_SCALE * DA_HD * bq * bk, bound_ref.shape)


def _modulation(cc, w_mod, b_mod, lq1, lk1, lq2, lk2, gq, gk):
    rows, d = cc.shape
    n3 = w_mod.shape[1]
    vec = pl.BlockSpec((1, DA_HD), lambda j: (0, 0))
    return pl.pallas_call(
        _mod_kernel,
        grid=(n3 // d,),
        in_specs=[pl.BlockSpec((rows, d), lambda j: (0, 0)),
                  pl.BlockSpec((d, d), lambda j: (0, j)),
                  pl.BlockSpec((1, d), lambda j: (0, j)),
                  vec, vec, vec, vec, vec, vec],
        out_specs=[pl.BlockSpec((rows, d), lambda j: (0, j)),
                   pl.BlockSpec((1, LANES), lambda j: (0, 0)),
                   pl.BlockSpec((1, LANES), lambda j: (0, 0))],
        out_shape=[jax.ShapeDtypeStruct((rows, n3), F32),
                   jax.ShapeDtypeStruct((1, LANES), F32),
                   jax.ShapeDtypeStruct((1, LANES), F32)],
        compiler_params=pltpu.CompilerParams(dimension_semantics=("arbitrary",),
                                             vmem_limit_bytes=VMEM_LIMIT),
        name="modulation",
    )(cc, w_mod, b_mod, lq1, lk1, lq2, lk2, gq, gk)


def _group_rms(acc, bd_ref, g_ref):
    ms = jnp.dot((acc * acc).astype(BF16), bd_ref[...], preferred_element_type=F32)
    return acc * lax.rsqrt(ms + EPS) * g_ref[...]


def _rope(x, cos, sin_signed, first_half):
    outs = []
    for c in range(x.shape[1] // LANES):
        xs = x[:, c * LANES:(c + 1) * LANES]
        partner = jnp.where(first_half, pltpu.roll(xs, LANES - 16, 1), pltpu.roll(xs, 16, 1))
        outs.append(xs * cos + partner * sin_signed)
    return jnp.concatenate(outs, axis=1)


def _inproj_kernel(x_ref, sc_ref, sh_ref, nw_ref, w_ref, wt_ref, wvt_ref, bif_ref, *rest, latent):
    if latent:
        (bd_ref, gq_ref, gk_ref, cos_ref, sin_ref,
         qa_ref, ka_ref, vt_ref, za_ref, xm_ref, zb_ref, ob_ref, gg_ref, gt_ref) = rest
    else:
        bd_ref, gk_ref, _, _, ka_ref, vt_ref, xm_ref, gt_ref = rest

    xf = x_ref[0]
    ms = jnp.mean(xf * xf, axis=-1, keepdims=True)
    y = xf * lax.rsqrt(ms + EPS) * nw_ref[...]
    h = (y * (1.0 + sc_ref[0]) + sh_ref[0]).astype(BF16)

    def proj(ref, c0, width):
        return jnp.dot(h, ref[:, c0:c0 + width], preferred_element_type=F32)

    vt = lax.dot_general(wvt_ref[...], h, (((1,), (1,)), ((), ())), preferred_element_type=F32).astype(BF16)
    for j in range(vt_ref.shape[1]):
        vt_ref[0, j] = vt[:, j * ATT_KC:(j + 1) * ATT_KC]

    if latent:
        tm = xf.shape[0]
        lane = lax.broadcasted_iota(jnp.int32, (tm, LANES), 1)
        first_half = (lane % 32) < 16
        cos = cos_ref[...]
        sin = sin_ref[...]
        q = _rope(_group_rms(proj(w_ref, COL_QA, DA_WIDTH), bd_ref, gq_ref), cos, sin, first_half)
        qa_ref[0] = (q * Q_SCALE).astype(BF16)
        k = _rope(_group_rms(proj(w_ref, COL_KA, DA_WIDTH), bd_ref, gk_ref), cos, sin, first_half)
        ka_ref[0] = k.astype(BF16)
        za_ref[0] = _silu(proj(w_ref, COL_ZA, DA_WIDTH)).astype(BF16)
        xm_ref[0] = proj(w_ref, COL_XM, ML_WIDTH).astype(BF16)
        zb_ref[0] = _silu(proj(w_ref, COL_ZB, ML_WIDTH)).astype(BF16)
        ob_ref[0] = _sigmoid(proj(w_ref, COL_OB, ML_WIDTH)).astype(BF16)
        for j in range(gg_ref.shape[2] // 512):
            gg_ref[0, :, j * 512:(j + 1) * 512] = _sigmoid(proj(wt_ref, j * 512, 512)).astype(BF16)
    else:
        ka_ref[0] = _group_rms(proj(w_ref, COL_KA, DA_WIDTH), bd_ref, gk_ref).astype(BF16)
        xm_ref[0] = proj(w_ref, COL_XM, ML_WIDTH).astype(BF16)

    g = proj(wt_ref, wt_ref.shape[1] - LANES, LANES).T[:ML_GATES] + bif_ref[...]
    row = lax.broadcasted_iota(jnp.int32, g.shape, 0)
    logsig = jnp.minimum(g, 0.0) - jnp.log(1.0 + jnp.exp(-jnp.abs(g)))
    gt_ref[0] = jnp.where((row // ML_HEADS) % 2 == 1, logsig, g)


def _in_proj(x, sc, sh, norm_w, w, w_tail, wvt, bif, extra, *, latent, n_keys, key_off):
    b, t, d = x.shape
    tm = min(IN_TILE, t)
    assert key_off % tm == 0
    kblk = key_off // tm
    grid = (b, t // tm)
    const2 = lambda bi, i: (0, 0)
    mod_map = (lambda bi, i: (bi, 0, 0)) if latent else (lambda bi, i: (0, 0, 0))
    tok = lambda width, dtype: (pl.BlockSpec((1, tm, width), lambda bi, i: (bi, i, 0)),
                                jax.ShapeDtypeStruct((b, t, width), dtype))
    keys = (pl.BlockSpec((1, tm, DA_WIDTH), lambda bi, i: (bi, kblk + i, 0)),
            jax.ShapeDtypeStruct((b, n_keys, DA_WIDTH), BF16))
    cpt = tm // ATT_KC
    w_tail_spec = (pl.BlockSpec(w_tail.shape, const2) if latent else
                   pl.BlockSpec((d, LANES), lambda bi, i: (0, w_tail.shape[1] // LANES - 1)))
    vals = (pl.BlockSpec((1, cpt, wvt.shape[0], ATT_KC), lambda bi, i: (bi, kblk + i, 0, 0)),
            jax.ShapeDtypeStruct((b, n_keys // ATT_KC, wvt.shape[0], ATT_KC), BF16))
    in_specs = [pl.BlockSpec((1, tm, d), lambda bi, i: (bi, i, 0)),
                pl.BlockSpec((1, 1, d), mod_map),
                pl.BlockSpec((1, 1, d), mod_map),
                pl.BlockSpec((1, d), const2),
                pl.BlockSpec((d, N_MAIN), const2),
                w_tail_spec,
                pl.BlockSpec(wvt.shape, const2),
                pl.BlockSpec(bif.shape, const2)]
    if latent:
        bd, gq, gk, cos, sin = extra
        in_specs += [pl.BlockSpec(bd.shape, const2), pl.BlockSpec(gq.shape, const2),
                     pl.BlockSpec(gk.shape, const2),
                     pl.BlockSpec((tm, LANES), lambda bi, i: (i, 0)),
                     pl.BlockSpec((tm, LANES), lambda bi, i: (i, 0))]
        outs = ([tok(DA_WIDTH, BF16), keys, vals, tok(DA_WIDTH, BF16)] + [tok(ML_WIDTH, BF16)] * 3
                + [tok(2 * d, BF16)])
        aliases = {}
    else:
        bd, gk, _, _ = extra
        in_specs += [pl.BlockSpec(bd.shape, const2), pl.BlockSpec(gk.shape, const2),
                     pl.BlockSpec(memory_space=pl.ANY), pl.BlockSpec(memory_space=pl.ANY)]
        outs = [keys, vals, tok(ML_WIDTH, BF16)]
        aliases = {len(in_specs) - 2: 0, len(in_specs) - 1: 1}
    outs.append((pl.BlockSpec((1, ML_GATES, tm), lambda bi, i: (bi, 0, i)),
                 jax.ShapeDtypeStruct((b, ML_GATES, t), F32)))
    return pl.pallas_call(
        functools.partial(_inproj_kernel, latent=latent),
        grid=grid,
        in_specs=in_specs,
        out_specs=[o[0] for o in outs],
        out_shape=[o[1] for o in outs],
        input_output_aliases=aliases,
        compiler_params=pltpu.CompilerParams(dimension_semantics=("parallel", "parallel"),
                                             vmem_limit_bytes=VMEM_LIMIT),
        name="in_proj_latent" if latent else "in_proj_context",
    )(x, sc, sh, norm_w, w, w_tail, wvt, bif, *extra)


def _attn_kernel(bound_ref, q_ref, k_ref, vt_ref, z_ref, lam_ref, hg_ref, o_ref, s_scr, e_scr, acc_scr, qm_scr):
    nblk = q_ref.shape[1] // ATT_BLK
    nkc = k_ref.shape[1] // ATT_KC
    sub = ATT_KC // 8
    lam = lam_ref[:, 0:1]
    nt = (((1,), (1,)), ((), ()))
    neg = jnp.full((8, ATT_BLK), -jnp.inf, F32)
    zero8 = jnp.zeros((8, ATT_BLK), F32)
    bound = bound_ref[0]

    def mask_queries(j):
        q = q_ref[0, j * ATT_BLK:(j + 1) * ATT_BLK, :]
        lane = lax.broadcasted_iota(jnp.int32, q.shape, 1)
        zero = jnp.zeros_like(q)
        qm_scr[0] = jnp.where(lane < DA_HD, q, zero)
        qm_scr[1] = jnp.where(lane >= DA_HD, q, zero)

    def scores(c, mp):
        kc = k_ref[0, pl.ds(pl.multiple_of(c * ATT_KC, ATT_KC), ATT_KC), :]
        return lax.dot_general(kc, qm_scr[mp], nt, preferred_element_type=F32).reshape(sub, 8, ATT_BLK)

    def exps(c, par, mp, st, stab8, lacc):
        e = jnp.exp2(st - stab8[None])
        e_scr[par, mp, c] = e.reshape(ATT_KC, ATT_BLK)
        return lacc + jnp.sum(e, axis=0)

    def values(c, par, r8):
        e1 = e_scr[par, 0, c].reshape(sub, 8, ATT_BLK)
        e2 = e_scr[par, 1, c].reshape(sub, 8, ATT_BLK)
        pt = (e1 - e2 * r8[None]).reshape(ATT_KC, ATT_BLK).astype(BF16)
        acc_scr[...] += jnp.dot(vt_ref[0, c], pt, preferred_element_type=F32)

    def finish(j, inv_l1):
        o = (acc_scr[...] * inv_l1).T
        ms = jnp.mean(o * o, axis=-1, keepdims=True)
        on = o * lax.rsqrt(ms + EPS) * hg_ref[...]
        rows = slice(j * ATT_BLK, (j + 1) * ATT_BLK)
        o_ref[0, rows, :] = (on * (1.0 - LAM_INIT) * z_ref[0, rows, :].astype(F32)).astype(BF16)

    def pipeline(exact_max):
        d_exp = 1 if exact_max else 0
        d_val = d_exp + 1
        bound8 = jnp.full((8, ATT_BLK), bound, F32)
        m8, r8, inv_l1 = {}, {}, {}
        for j in range(nblk + d_val):
            do_s, do_e, do_v = j < nblk, d_exp <= j < nblk + d_exp, d_val <= j
            if do_s:
                mask_queries(j)
            if do_v:
                acc_scr[...] = jnp.zeros_like(acc_scr)

            def body(c, carry, j=j, do_s=do_s, do_e=do_e, do_v=do_v):
                macc, lacc = carry
                if exact_max:
                    if do_s:
                        macc = list(macc)
                        for mp in range(2):
                            st = scores(c, mp)
                            s_scr[j % 2, mp, c] = st.reshape(ATT_KC, ATT_BLK)
                            macc[mp] = jnp.maximum(macc[mp], jnp.max(st, axis=0))
                        macc = tuple(macc)
                    if do_e:
                        jb = j - d_exp
                        lacc = tuple(
                            exps(c, jb % 2, mp, s_scr[jb % 2, mp, c].reshape(sub, 8, ATT_BLK), m8[jb][mp], lacc[mp])
                            for mp in range(2))
                elif do_s:
                    lacc = tuple(exps(c, j % 2, mp, scores(c, mp), bound8, lacc[mp]) for mp in range(2))
                if do_v:
                    values(c, (j - d_val) % 2, r8[j - d_val])
                return macc, lacc

            macc, lacc = lax.fori_loop(0, nkc, body, ((neg, neg), (zero8, zero8)),
                                       unroll=ATT_UNROLL if exact_max else nkc)
            if do_v:
                finish(j - d_val, inv_l1.pop(j - d_val))
                r8.pop(j - d_val)
            if do_e:
                l1 = jnp.sum(lacc[0], axis=0, keepdims=True)
                l2 = jnp.sum(lacc[1], axis=0, keepdims=True)
                r8[j - d_exp] = jnp.broadcast_to(lam * l1 / l2, (8, ATT_BLK))
                inv_l1[j - d_exp] = 1.0 / l1
                m8.pop(j - d_exp, None)
            if exact_max and do_s:
                m8[j] = tuple(jnp.broadcast_to(jnp.max(a, axis=0, keepdims=True), (8, ATT_BLK)) for a in macc)

    use_bound = bound <= 0.5 * (126.0 - ATT_EXP2_HEADROOM)

    @pl.when(use_bound)
    def _():
        pipeline(False)

    @pl.when(jnp.logical_not(use_bound))
    def _():
        pipeline(True)


def _diff_attn(bound, qa, k_all, vt_all, za, lam, head_g):
    b, t, _ = qa.shape
    n_keys = k_all.shape[1]
    nkc = n_keys // ATT_KC
    assert t % ATT_BLK == 0 and n_keys % ATT_KC == 0 and nkc % ATT_UNROLL == 0
    qmap = lambda bi, h: (bi, 0, h)
    const2 = lambda bi, h: (0, 0)
    return pl.pallas_call(
        _attn_kernel,
        grid=(b, DA_HEADS),
        in_specs=[pl.BlockSpec(memory_space=pltpu.SMEM),
                  pl.BlockSpec((1, t, DA_HEAD_W), qmap),
                  pl.BlockSpec((1, n_keys, DA_HEAD_W), qmap),
                  pl.BlockSpec((1, nkc, DA_HEAD_W, ATT_KC), lambda bi, h: (bi, 0, h, 0)),
                  pl.BlockSpec((1, t, DA_HEAD_W), qmap),
                  pl.BlockSpec((1, LANES), const2),
                  pl.BlockSpec((1, DA_HEAD_W), const2)],
        out_specs=pl.BlockSpec((1, t, DA_HEAD_W), qmap),
        out_shape=jax.ShapeDtypeStruct((b, t, DA_WIDTH), BF16),
        scratch_shapes=[pltpu.VMEM((2, 2, nkc, ATT_KC, ATT_BLK), F32),
                        pltpu.VMEM((2, 2, nkc, ATT_KC, ATT_BLK), F32),
                        pltpu.VMEM((DA_HEAD_W, ATT_BLK), F32),
                        pltpu.VMEM((2, ATT_BLK, DA_HEAD_W), BF16)],
        compiler_params=pltpu.CompilerParams(dimension_semantics=("parallel", "arbitrary"),
                                             vmem_limit_bytes=VMEM_LIMIT),
        name="diff_attn",
    )(bound, qa, k_all, vt_all, za, lam, head_g)


def _conv_silu(x, w, b):
    t = x.shape[0]
    row = lax.broadcasted_iota(jnp.int32, x.shape, 0)
    prev = jnp.where(row == 0, 0.0, pltpu.roll(x, 1, 0))
    nxt = jnp.where(row == t - 1, 0.0, pltpu.roll(x, t - 1, 0))
    y = b + w[0:1, :] * prev
    y = y + w[1:2, :] * x
    y = y + w[2:3, :] * nxt
    return _silu(y)


def _rows_to_cols(rows, eye):
    nt = (((1,), (1,)), ((), ()))
    out = None
    for part in _split3(rows):
        term = lax.dot_general(eye, part, nt, preferred_element_type=F32)
        out = term if out is None else out + term
    return out


def _cumsum_rows(rows, tri):
    out = None
    for part in _split3(rows):
        term = jnp.dot(part, tri, preferred_element_type=F32)
        out = term if out is None else out + term
    return out


def _cummax_rows(rows, reverse):
    n = rows.shape[1]
    lane = lax.broadcasted_iota(jnp.int32, rows.shape, 1)
    sh = 1
    while sh < n:
        if reverse:
            shifted = jnp.where(lane < n - sh, pltpu.roll(rows, n - sh, 1), -jnp.inf)
        else:
            shifted = jnp.where(lane >= sh, pltpu.roll(rows, sh, 1), -jnp.inf)
        rows = jnp.maximum(rows, shifted)
        sh *= 2
    return rows


def _mlstm_kernel(xl_ref, xc_ref, vt_ref, gl_ref, gc_ref, zb_ref, ob_ref,
                  cw_ref, cb_ref, wqt_ref, wk_ref, hg_ref, sk_ref, o_ref,
                  xcv_s, qt_s, k_s, vat_s, hf_s, hb_s, st_s, gate_s, bcol_s):
    hp = pl.program_id(1)
    n_lat = xl_ref.shape[1]
    nchunk = n_lat // ML_L
    L = ML_L
    nt = (((1,), (1,)), ((), ()))

    r_i = lax.broadcasted_iota(jnp.int32, (L, L), 0)
    c_i = lax.broadcasted_iota(jnp.int32, (L, L), 1)
    eye = jnp.where(r_i == c_i, 1.0, 0.0).astype(BF16)
    tris = (jnp.where(r_i <= c_i, 1.0, 0.0).astype(BF16), jnp.where(r_i >= c_i, 1.0, 0.0).astype(BF16))
    visible = (r_i <= c_i, r_i >= c_i)
    row = lax.broadcasted_iota(jnp.int32, (VA_ROWS - ML_HD, L), 0)
    ones_row = jnp.where(row == 0, 1.0, 0.0).astype(BF16)
    lane8 = lax.broadcasted_iota(jnp.int32, (8, L), 1)

    for hh in range(ML_HPP):
        head = hp * ML_HPP + hh
        cols_h = slice(hh * ML_HD, (hh + 1) * ML_HD)
        wqt = wqt_ref[hh]
        wk = wk_ref[hh]

        def prepare(x_ref, slot0, hh=hh, cols_h=cols_h, wqt=wqt, wk=wk):
            n = x_ref.shape[1]
            xc = _conv_silu(x_ref[0, :, cols_h].astype(F32), cw_ref[:, cols_h], cb_ref[:, cols_h])
            xb = xc.astype(BF16)
            qt = lax.dot_general(wqt, xb, nt, preferred_element_type=F32).astype(BF16)
            for j in range(n // L):
                qt_s[hh, slot0 + j] = qt[:, j * L:(j + 1) * L]
            k = jnp.dot(xb, wk, preferred_element_type=F32) * (ML_HD ** -0.5)
            k_s[hh, slot0 * L:slot0 * L + n, :] = k.astype(BF16)
            return xc

        prepare(xc_ref, nchunk)
        xcv_s[hh] = prepare(xl_ref, 0)
        for j in range(nchunk + 1):
            vat_s[hh, j, 0:ML_HD, :] = vt_ref[0, j, cols_h, :]
            vat_s[hh, j, ML_HD:VA_ROWS, :] = ones_row

        for d in range(2):
            for g_ref, r0 in ((gl_ref, 0), (gc_ref, nchunk)):
                ig = g_ref[0, (2 * d) * ML_HEADS + head] * LOG2E
                lf = g_ref[0, (2 * d + 1) * ML_HEADS + head] * LOG2E
                n = ig.shape[0]
                if n == 1:
                    ig = jnp.broadcast_to(ig, (8, L))
                    lf = jnp.broadcast_to(lf, (8, L))
                a = _cumsum_rows(lf, tris[d])
                bb = ig - a
                pick = (lane8 == L - 1) if d == 0 else (lane8 == 0)
                f_tot = jnp.sum(jnp.where(pick, a, 0.0), axis=-1, keepdims=True)
                g0 = 4 * d * GATE_ROWS + r0
                gate_s[hh, g0:g0 + n, :] = a[0:n]
                gate_s[hh, g0 + GATE_ROWS:g0 + GATE_ROWS + n, :] = _cummax_rows(bb, d == 1)[0:n]
                gate_s[hh, g0 + 2 * GATE_ROWS:g0 + 2 * GATE_ROWS + n, :] = jnp.broadcast_to(f_tot, (8, L))[0:n]
                gate_s[hh, g0 + 3 * GATE_ROWS:g0 + 3 * GATE_ROWS + n, :] = jnp.broadcast_to(
                    jnp.max(bb, axis=-1, keepdims=True), (8, L))[0:n]
                cols = _rows_to_cols(bb, eye)
                for j in range(n):
                    bcol_s[hh * 2 + d, r0 + j] = jnp.broadcast_to(cols[:, j:j + 1], (L, ML_HD))

    def gate_row(hh, d, kind, slot):
        return gate_s[hh, pl.ds((4 * d + kind) * GATE_ROWS + slot, 1), :]

    def chunk_step(hh, d, slot, m_old, h_s, first):
        sd = hh * 2 + d
        f_tot = gate_row(hh, d, 2, slot)[:, 0:1]
        b_max = gate_row(hh, d, 3, slot)[:, 0:1]
        bcol = bcol_s[sd, slot]
        kc = k_s[hh, pl.ds(pl.multiple_of(slot * L, L), L), :]
        vat = vat_s[hh, slot]
        if not first:
            qt = qt_s[hh, slot]
            mm = jnp.maximum(gate_row(hh, d, 1, slot), m_old)
            logd = jnp.where(visible[d], jnp.concatenate([bcol] * (L // ML_HD), axis=1) - mm, -jnp.inf)
            pt = (jnp.dot(kc, qt, preferred_element_type=F32) * jnp.exp2(logd)).astype(BF16)
            tot = (jnp.dot(vat, pt, preferred_element_type=F32)
                   + jnp.exp2(m_old - mm) * jnp.dot(st_s[sd].astype(BF16), qt, preferred_element_type=F32))
            den = tot[ML_HD:ML_HD + 1, :]
            floor = jnp.exp2(-(gate_row(hh, d, 0, slot) + mm))
            h_s[hh, slot] = tot[0:ML_HD, :] * (1.0 / jnp.maximum(jnp.abs(den), floor))
        m_new = jnp.maximum(f_tot + m_old, f_tot + b_max)
        kw = (kc.astype(F32) * jnp.exp2(bcol + (f_tot - m_new))).astype(BF16)
        upd = jnp.dot(vat, kw, preferred_element_type=F32)
        if first:
            st_s[sd] = upd
        else:
            st_s[sd] = jnp.exp2(f_tot + m_old - m_new) * st_s[sd] + upd
        return m_new

    zero11 = jnp.zeros((1, 1), F32)
    m0 = tuple(chunk_step(hh, d, nchunk, zero11, None, True) for hh in range(ML_HPP) for d in range(2))

    def body(i, ms):
        out = []
        for hh in range(ML_HPP):
            out.append(chunk_step(hh, 0, i, ms[2 * hh], hf_s, False))
            out.append(chunk_step(hh, 1, nchunk - 1 - i, ms[2 * hh + 1], hb_s, False))
        return tuple(out)

    lax.fori_loop(0, nchunk, body, m0)

    for hh in range(ML_HPP):
        cols_h = slice(hh * ML_HD, (hh + 1) * ML_HD)
        for c in range(nchunk):
            ht = hf_s[hh, c] + hb_s[hh, c]
            ms = jnp.mean(ht * ht, axis=0, keepdims=True)
            hn = (ht * lax.rsqrt(ms + EPS)).T * hg_ref[:, cols_h]
            rows = slice(c * L, (c + 1) * L)
            y = ((ob_ref[0, rows, cols_h].astype(F32) * hn + sk_ref[:, cols_h] * xcv_s[hh, rows, :])
                 * zb_ref[0, rows, cols_h].astype(F32))
            o_ref[0, rows, cols_h] = y.astype(BF16)


def _mlstm(xm_l, xm_c, vt_all, gt_l, gt_c, zb, ob, conv_w, conv_b, wqt, wk, head_g, skip):
    b, t, _ = xm_l.shape
    n_ctx = xm_c.shape[1]
    nchunk = t // ML_L
    nslot = nchunk + 1
    assert t % (8 * ML_L) == 0 and n_ctx == ML_L and nchunk < GATE_ROWS and vt_all.shape[1] == nslot
    gl = gt_l.reshape(b, ML_GATES, nchunk, ML_L)
    gc = gt_c.reshape(b, ML_GATES, 1, ML_L)
    pw = ML_HPP * ML_HD
    vt_blk0 = DA_WIDTH // pw
    tokmap = lambda bi, h: (bi, 0, h)
    gmap = lambda bi, h: (bi, 0, 0, 0)
    hvec = lambda bi, h: (0, h)
    wmap = lambda bi, h: (h, 0, 0)
    return pl.pallas_call(
        _mlstm_kernel,
        grid=(b, ML_HEADS // ML_HPP),
        in_specs=[pl.BlockSpec((1, t, pw), tokmap),
                  pl.BlockSpec((1, n_ctx, pw), tokmap),
                  pl.BlockSpec((1, nslot, pw, ML_L), lambda bi, h: (bi, 0, vt_blk0 + h, 0)),
                  pl.BlockSpec((1, ML_GATES, nchunk, ML_L), gmap),
                  pl.BlockSpec((1, ML_GATES, 1, ML_L), gmap),
                  pl.BlockSpec((1, t, pw), tokmap),
                  pl.BlockSpec((1, t, pw), tokmap),
                  pl.BlockSpec((conv_w.shape[0], pw), hvec),
                  pl.BlockSpec((1, pw), hvec),
                  pl.BlockSpec((ML_HPP, ML_HD, ML_HD), wmap),
                  pl.BlockSpec((ML_HPP, ML_HD, ML_HD), wmap),
                  pl.BlockSpec((1, pw), hvec),
                  pl.BlockSpec((1, pw), hvec)],
        out_specs=pl.BlockSpec((1, t, pw), tokmap),
        out_shape=jax.ShapeDtypeStruct((b, t, ML_WIDTH), BF16),
        scratch_shapes=[pltpu.VMEM((ML_HPP, t, ML_HD), F32),
                        pltpu.VMEM((ML_HPP, nslot, ML_HD, ML_L), BF16),
                        pltpu.VMEM((ML_HPP, nslot * ML_L, ML_HD), BF16),
                        pltpu.VMEM((ML_HPP, nslot, VA_ROWS, ML_L), BF16),
                        pltpu.VMEM((ML_HPP, nchunk, ML_HD, ML_L), F32),
                        pltpu.VMEM((ML_HPP, nchunk, ML_HD, ML_L), F32),
                        pltpu.VMEM((ML_HPP * 2, VA_ROWS, ML_HD), F32),
                        pltpu.VMEM((ML_HPP, 8 * GATE_ROWS, ML_L), F32),
                        pltpu.VMEM((ML_HPP * 2, nslot, ML_L, ML_HD), F32)],
        compiler_params=pltpu.CompilerParams(dimension_semantics=("parallel", "arbitrary"),
                                             vmem_limit_bytes=VMEM_LIMIT),
        name="mlstm",
    )(xm_l, xm_c, vt_all, gl, gc, zb, ob, conv_w, conv_b, wqt, wk, head_g, skip)


def _outproj_kernel(x_ref, ya_ref, yb_ref, gg_ref, gt_ref, woa_ref, wob_ref, wo_ref, o_ref):
    d = x_ref.shape[2]
    a = jnp.dot(ya_ref[0], woa_ref[...], preferred_element_type=F32)
    b = jnp.dot(yb_ref[0], wob_ref[...], preferred_element_type=F32)
    y = gg_ref[0, :, 0:d].astype(F32) * a + gg_ref[0, :, d:2 * d].astype(F32) * b
    o = jnp.dot(y.astype(BF16), wo_ref[...], preferred_element_type=F32)
    o_ref[0] = x_ref[0] + gt_ref[0] * o


def _out_proj(x, ya, yb, gg, gt, woa, wob, wo):
    b, t, d = x.shape
    tm = min(IN_TILE, t)
    tok = lambda width: pl.BlockSpec((1, tm, width), lambda bi, i: (bi, i, 0))
    const2 = lambda bi, i: (0, 0)
    return pl.pallas_call(
        _outproj_kernel,
        grid=(b, t // tm),
        in_specs=[tok(d), tok(DA_WIDTH), tok(ML_WIDTH), tok(2 * d),
                  pl.BlockSpec((1, 1, d), lambda bi, i: (bi, 0, 0)),
                  pl.BlockSpec(woa.shape, const2), pl.BlockSpec(wob.shape, const2),
                  pl.BlockSpec(wo.shape, const2)],
        out_specs=tok(d),
        out_shape=jax.ShapeDtypeStruct((b, t, d), x.dtype),
        compiler_params=pltpu.CompilerParams(dimension_semantics=("parallel", "parallel"),
                                             vmem_limit_bytes=VMEM_LIMIT),
        name="out_proj",
    )(x, ya, yb, gg, gt, woa, wob, wo)


def _rope_tables(n_tokens):
    rows = n_tokens // GRID_W
    row_id = jnp.repeat(jnp.arange(rows, dtype=F32), GRID_W)
    col_id = jnp.tile(jnp.arange(GRID_W, dtype=F32), rows)
    n_freq = DA_HD // 4
    inv_freq = ROPE_THETA ** (-jnp.arange(n_freq, dtype=F32) / n_freq)
    ang_r = row_id[:, None] * inv_freq
    ang_c = col_id[:, None] * inv_freq
    cos = jnp.concatenate([jnp.cos(ang_r)] * 2 + [jnp.cos(ang_c)] * 2, axis=-1)
    sin = jnp.concatenate([-jnp.sin(ang_r), jnp.sin(ang_r), -jnp.sin(ang_c), jnp.sin(ang_c)], axis=-1)
    return jnp.tile(cos, (1, LANES // DA_HD)), jnp.tile(sin, (1, LANES // DA_HD))


def kernel(x, c, ctx, c_ctx, norm_w, w_mod, b_mod, w_in, b_if, da_q_norm, da_k_norm, da_lambda_q1, da_lambda_k1, da_lambda_q2, da_lambda_k2, da_head_norm, w_out_a, ml_conv_w, ml_conv_b, ml_wq, ml_wk, ml_head_norm, ml_skip, w_out_b, w_o):
    assert w_mod.shape[0] == 1, "single-layer block"
    b, t, d = x.shape

    rows = -(-(b + 1) // 8) * 8
    cc = jnp.zeros((rows, d), F32).at[:b].set(c).at[b].set(c_ctx)
    mod, lam, bound = _modulation(cc, w_mod[0], b_mod, da_lambda_q1, da_lambda_k1, da_lambda_q2, da_lambda_k2,
                                  da_q_norm, da_k_norm)
    sh_l, sc_l, gt_l = (mod[:b, i * d:(i + 1) * d].reshape(b, 1, d) for i in range(3))
    sh_c, sc_c = (mod[b:b + 1, i * d:(i + 1) * d].reshape(1, 1, d) for i in range(2))

    w = w_in[0]
    w_main = w.astype(BF16)
    w_gates = jnp.pad(w[:, N_MAIN:N_MAIN + ML_GATES], ((0, 0), (0, LANES - ML_GATES)))
    w_tail = jnp.concatenate([w[:, N_MAIN + ML_GATES:], w_gates], axis=1).astype(BF16)
    w_vt = jnp.concatenate([w_main[:, COL_VA:COL_ZA], w_main[:, COL_VM:COL_ZB]], axis=1).T
    bif = b_if.reshape(ML_GATES, 1)

    grp = np.arange(DA_WIDTH) // DA_HD
    bd = jnp.asarray((grp[:, None] == grp[None, :]).astype(np.float32) / DA_HD, dtype=BF16)
    gq = jnp.tile(da_q_norm, (1, DA_WIDTH // DA_HD))
    gk = jnp.tile(da_k_norm, (1, DA_WIDTH // DA_HD))
    cos, sin = _rope_tables(t)

    n_keys = t + ctx.shape[1]
    qa, k_all, vt_all, za, xm_l, zb, ob, gg, gtl = _in_proj(
        x, sc_l, sh_l, norm_w, w_main, w_tail, w_vt, bif, (bd, gq, gk, cos, sin), latent=True, n_keys=n_keys, key_off=0)
    k_all, vt_all, xm_c, gtc = _in_proj(
        ctx, sc_c, sh_c, norm_w, w_main, w_tail, w_vt, bif, (bd, gk, k_all, vt_all), latent=False, n_keys=n_keys, key_off=t)

    ya = _diff_attn(bound[0, :1], qa, k_all, vt_all, za, lam, da_head_norm)
    yb = _mlstm(xm_l, xm_c, vt_all, gtl, gtc, zb, ob, ml_conv_w[0], ml_conv_b,
                jnp.swapaxes(ml_wq[0], 1, 2).astype(BF16), ml_wk[0].astype(BF16), ml_head_norm, ml_skip)
    return _out_proj(x, ya, yb, gg, gt_l, w_out_a[0].astype(BF16), w_out_b[0].astype(BF16),
                     w_o[0].astype(BF16))
```

```python
import functools

import jax
import jax.numpy as jnp
import numpy as np
from jax import lax
from jax.experimental import pallas as pl
from jax.experimental.pallas import tpu as pltpu

F32 = jnp.float32
BF16 = jnp.bfloat16

EPS = 1e-6
ROPE_THETA = 10000.0
GRID_W = 64
LAM_INIT = 0.8 - 0.6 * 1.0

DA_HEADS = 4
DA_HD = 64
DA_HEAD_W = 2 * DA_HD
DA_WIDTH = DA_HEADS * DA_HEAD_W

ML_HEADS = 4
ML_HD = 128
ML_WIDTH = ML_HEADS * ML_HD
ML_GATES = 4 * ML_HEADS

COL_QA, COL_KA, COL_VA, COL_ZA = (i * DA_WIDTH for i in range(4))
COL_XM, COL_VM, COL_ZB, COL_OB = (4 * DA_WIDTH + i * ML_WIDTH for i in range(4))
N_MAIN = 4 * DA_WIDTH + 4 * ML_WIDTH
ML_L = 256
VA_ROWS = ML_HD + 16
LOG2E = 1.4426950408889634
ML_HPP = 2

LANES = 128
VMEM_LIMIT = 56 * 1024 * 1024

IN_TILE = 512
OUT_TILE = 1024
ATT_BLK = 256
ATT_KC = 256
ATT_UNROLL = 3
ATT_EXP2_HEADROOM = 30.0
SCORE_BOUND_MARGIN = 1.02
Q_SCALE = DA_HD ** -0.5 * LOG2E


def _sigmoid(x):
    return 1.0 / (1.0 + jnp.exp(-x))


def _silu(x):
    return x * _sigmoid(x)


def _split3(x):
    hi = x.astype(BF16)
    r1 = x - hi.astype(F32)
    mid = r1.astype(BF16)
    lo = (r1 - mid.astype(F32)).astype(BF16)
    return hi, mid, lo


def _mod_kernel(cc_ref, w_ref, b_ref, lq1_ref, lk1_ref, lq2_ref, lk2_ref, gq_ref, gk_ref,
                mod_ref, lam_ref, bound_ref):
    a = _silu(cc_ref[...]).astype(BF16)
    mod_ref[...] = jnp.dot(a, w_ref[...].astype(BF16), preferred_element_type=F32) + b_ref[...]
    s1 = jnp.sum(lq1_ref[...] * lk1_ref[...], axis=-1, keepdims=True)
    s2 = jnp.sum(lq2_ref[...] * lk2_ref[...], axis=-1, keepdims=True)
    lam = jnp.exp(s1) - jnp.exp(s2) + LAM_INIT
    lam_ref[...] = jnp.broadcast_to(lam, lam_ref.shape)
    bq = jnp.max(jnp.abs(gq_ref[...]), axis=-1, keepdims=True)
    bk = jnp.max(jnp.abs(gk_ref[...]), axis=-1, keepdims=True)
    bound_ref[...] = jnp.broadcast_to(SCORE_BOUND_MARGIN * Q_SCALE * DA_HD * bq * bk, bound_ref.shape)


def _modulation(cc, w_mod, b_mod, lq1, lk1, lq2, lk2, gq, gk):
    rows, d = cc.shape
    n3 = w_mod.shape[1]
    vec = pl.BlockSpec((1, DA_HD), lambda j: (0, 0))
    return pl.pallas_call(
        _mod_kernel,
        grid=(n3 // d,),
        in_specs=[pl.BlockSpec((rows, d), lambda j: (0, 0)),
                  pl.BlockSpec((d, d), lambda j: (0, j)),
                  pl.BlockSpec((1, d), lambda j: (0, j)),
                  vec, vec, vec, vec, vec, vec],
        out_specs=[pl.BlockSpec((rows, d), lambda j: (0, j)),
                   pl.BlockSpec((1, LANES), lambda j: (0, 0)),
                   pl.BlockSpec((1, LANES), lambda j: (0, 0))],
        out_shape=[jax.ShapeDtypeStruct((rows, n3), F32),
                   jax.ShapeDtypeStruct((1, LANES), F32),
                   jax.ShapeDtypeStruct((1, LANES), F32)],
        compiler_params=pltpu.CompilerParams(dimension_semantics=("arbitrary",),
                                             vmem_limit_bytes=VMEM_LIMIT),
        name="modulation",
    )(cc, w_mod, b_mod, lq1, lk1, lq2, lk2, gq, gk)


def _group_rms(acc, bd_ref, g_ref):
    ms = jnp.dot((acc * acc).astype(BF16), bd_ref[...], preferred_element_type=F32)
    return acc * lax.rsqrt(ms + EPS) * g_ref[...]


def _rope(x, cos, sin_signed, first_half):
    outs = []
    for c in range(x.shape[1] // LANES):
        xs = x[:, c * LANES:(c + 1) * LANES]
        partner = jnp.where(first_half, pltpu.roll(xs, LANES - 16, 1), pltpu.roll(xs, 16, 1))
        outs.append(xs * cos + partner * sin_signed)
    return jnp.concatenate(outs, axis=1)


def _inproj_kernel(x_ref, sc_ref, sh_ref, nw_ref, w_ref, wvt_ref, bif_ref, *rest, latent):
    if latent:
        (wt_ref, bd_ref, gq_ref, gk_ref, cos_ref, sin_ref,
         qa_ref, ka_ref, vt_ref, za_ref, xm_ref, zb_ref, ob_ref, gg_ref, gt_ref) = rest
    else:
        bd_ref, gk_ref, _, _, ka_ref, vt_ref, xm_ref, gt_ref = rest

    xf = x_ref[0]
    ms = jnp.mean(xf * xf, axis=-1, keepdims=True)
    y = xf * lax.rsqrt(ms + EPS) * nw_ref[...]
    h = (y * (1.0 + sc_ref[0]) + sh_ref[0]).astype(BF16)

    def proj(ref, c0, width):
        return jnp.dot(h, ref[:, c0:c0 + width], preferred_element_type=F32)

    ft = lax.dot_general(wvt_ref[...], h, (((1,), (1,)), ((), ())), preferred_element_type=F32)
    n_val = vt_ref.shape[2]
    vt = ft[0:n_val].astype(BF16)
    for j in range(vt_ref.shape[1]):
        vt_ref[0, j] = vt[:, j * ATT_KC:(j + 1) * ATT_KC]

    if latent:
        tm = xf.shape[0]
        lane = lax.broadcasted_iota(jnp.int32, (tm, LANES), 1)
        first_half = (lane % 32) < 16
        cos = cos_ref[...]
        sin = sin_ref[...]
        q = _rope(_group_rms(proj(w_ref, COL_QA, DA_WIDTH), bd_ref, gq_ref), cos, sin, first_half)
        qa_ref[0] = (q * Q_SCALE).astype(BF16)
        k = _rope(_group_rms(proj(w_ref, COL_KA, DA_WIDTH), bd_ref, gk_ref), cos, sin, first_half)
        ka_ref[0] = k.astype(BF16)
        za_ref[0] = _silu(proj(w_ref, COL_ZA, DA_WIDTH)).astype(BF16)
        xm_ref[0] = proj(w_ref, COL_XM, ML_WIDTH).astype(BF16)
        zb_ref[0] = _silu(proj(w_ref, COL_ZB, ML_WIDTH)).astype(BF16)
        ob_ref[0] = _sigmoid(proj(w_ref, COL_OB, ML_WIDTH)).astype(BF16)
        for j in range(gg_ref.shape[2] // 512):
            gg_ref[0, :, j * 512:(j + 1) * 512] = _sigmoid(proj(wt_ref, j * 512, 512)).astype(BF16)
    else:
        ka_ref[0] = _group_rms(proj(w_ref, COL_KA, DA_WIDTH), bd_ref, gk_ref).astype(BF16)
        xm_ref[0] = proj(w_ref, COL_XM, ML_WIDTH).astype(BF16)

    g = ft[n_val:n_val + ML_GATES] + bif_ref[...]
    row = lax.broadcasted_iota(jnp.int32, g.shape, 0)
    logsig = jnp.minimum(g, 0.0) - jnp.log(1.0 + jnp.exp(-jnp.abs(g)))
    gt_ref[0] = jnp.where((row // ML_HEADS) % 2 == 1, logsig, g)


def _in_proj(x, sc, sh, norm_w, w, wvt, bif, extra, *, latent, n_keys, key_off):
    b, t, d = x.shape
    tm = min(IN_TILE, t)
    assert key_off % tm == 0
    kblk = key_off // tm
    grid = (b, t // tm)
    const2 = lambda bi, i: (0, 0)
    mod_map = (lambda bi, i: (bi, 0, 0)) if latent else (lambda bi, i: (0, 0, 0))
    tok = lambda width, dtype: (pl.BlockSpec((1, tm, width), lambda bi, i: (bi, i, 0)),
                                jax.ShapeDtypeStruct((b, t, width), dtype))
    keys = (pl.BlockSpec((1, tm, DA_WIDTH), lambda bi, i: (bi, kblk + i, 0)),
            jax.ShapeDtypeStruct((b, n_keys, DA_WIDTH), BF16))
    cpt = tm // ATT_KC
    n_val = wvt.shape[0] - ML_GATES
    vals = (pl.BlockSpec((1, cpt, n_val, ATT_KC), lambda bi, i: (bi, kblk + i, 0, 0)),
            jax.ShapeDtypeStruct((b, n_keys // ATT_KC, n_val, ATT_KC), BF16))
    in_specs = [pl.BlockSpec((1, tm, d), lambda bi, i: (bi, i, 0)),
                pl.BlockSpec((1, 1, d), mod_map),
                pl.BlockSpec((1, 1, d), mod_map),
                pl.BlockSpec((1, d), const2),
                pl.BlockSpec((d, N_MAIN), const2),
                pl.BlockSpec(wvt.shape, const2),
                pl.BlockSpec(bif.shape, const2)]
    if latent:
        w_tail, bd, gq, gk, cos, sin = extra
        in_specs += [pl.BlockSpec(w_tail.shape, const2),
                     pl.BlockSpec(bd.shape, const2), pl.BlockSpec(gq.shape, const2),
                     pl.BlockSpec(gk.shape, const2),
                     pl.BlockSpec((tm, LANES), lambda bi, i: (i, 0)),
                     pl.BlockSpec((tm, LANES), lambda bi, i: (i, 0))]
        outs = ([tok(DA_WIDTH, BF16), keys, vals, tok(DA_WIDTH, BF16)] + [tok(ML_WIDTH, BF16)] * 3
                + [tok(2 * d, BF16)])
        aliases = {}
    else:
        bd, gk, _, _ = extra
        in_specs += [pl.BlockSpec(bd.shape, const2), pl.BlockSpec(gk.shape, const2),
                     pl.BlockSpec(memory_space=pl.ANY), pl.BlockSpec(memory_space=pl.ANY)]
        outs = [keys, vals, tok(ML_WIDTH, BF16)]
        aliases = {len(in_specs) - 2: 0, len(in_specs) - 1: 1}
    outs.append((pl.BlockSpec((1, ML_GATES, tm), lambda bi, i: (bi, 0, i)),
                 jax.ShapeDtypeStruct((b, ML_GATES, t), F32)))
    return pl.pallas_call(
        functools.partial(_inproj_kernel, latent=latent),
        grid=grid,
        in_specs=in_specs,
        out_specs=[o[0] for o in outs],
        out_shape=[o[1] for o in outs],
        input_output_aliases=aliases,
        compiler_params=pltpu.CompilerParams(dimension_semantics=("parallel", "parallel"),
                                             vmem_limit_bytes=VMEM_LIMIT),
        name="in_proj_latent" if latent else "in_proj_context",
    )(x, sc, sh, norm_w, w, wvt, bif, *extra)


def _attn_kernel(bound_ref, q_ref, k_ref, vt_ref, z_ref, lam_ref, hg_ref, o_ref, s_scr, e_scr, acc_scr, qm_scr):
    nblk = q_ref.shape[1] // ATT_BLK
    nkc = k_ref.shape[1] // ATT_KC
    sub = ATT_KC // 8
    lam = lam_ref[:, 0:1]
    nt = (((1,), (1,)), ((), ()))
    neg = jnp.full((8, ATT_BLK), -jnp.inf, F32)
    zero8 = jnp.zeros((8, ATT_BLK), F32)
    bound = bound_ref[0]

    def mask_queries(j):
        q = q_ref[0, j * ATT_BLK:(j + 1) * ATT_BLK, :]
        lane = lax.broadcasted_iota(jnp.int32, q.shape, 1)
        zero = jnp.zeros_like(q)
        qm_scr[0] = jnp.where(lane < DA_HD, q, zero)
        qm_scr[1] = jnp.where(lane >= DA_HD, q, zero)

    def scores(c, mp):
        kc = k_ref[0, pl.ds(pl.multiple_of(c * ATT_KC, ATT_KC), ATT_KC), :]
        return lax.dot_general(kc, qm_scr[mp], nt, preferred_element_type=F32).reshape(sub, 8, ATT_BLK)

    def exps(c, par, mp, st, stab8, lacc):
        e = jnp.exp2(st - stab8[None])
        e_scr[par, mp, c] = e.reshape(ATT_KC, ATT_BLK)
        return lacc + jnp.sum(e, axis=0)

    def values(c, par, r8):
        e1 = e_scr[par, 0, c].reshape(sub, 8, ATT_BLK)
        e2 = e_scr[par, 1, c].reshape(sub, 8, ATT_BLK)
        pt = (e1 - e2 * r8[None]).reshape(ATT_KC, ATT_BLK).astype(BF16)
        acc_scr[...] += jnp.dot(vt_ref[0, c], pt, preferred_element_type=F32)

    def finish(j, inv_l1):
        o = (acc_scr[...] * inv_l1).T
        ms = jnp.mean(o * o, axis=-1, keepdims=True)
        on = o * lax.rsqrt(ms + EPS) * hg_ref[...]
        rows = slice(j * ATT_BLK, (j + 1) * ATT_BLK)
        o_ref[0, rows, :] = (on * (1.0 - LAM_INIT) * z_ref[0, rows, :].astype(F32)).astype(BF16)

    def pipeline(exact_max):
        d_exp = 1 if exact_max else 0
        d_val = d_exp + 1
        bound8 = jnp.full((8, ATT_BLK), bound, F32)
        m8, r8, inv_l1 = {}, {}, {}
        for j in range(nblk + d_val):
            do_s, do_e, do_v = j < nblk, d_exp <= j < nblk + d_exp, d_val <= j
            if do_s:
                mask_queries(j)
            if do_v:
                acc_scr[...] = jnp.zeros_like(acc_scr)

            def body(c, carry, j=j, do_s=do_s, do_e=do_e, do_v=do_v):
                macc, lacc = carry
                if exact_max:
                    if do_s:
                        macc = list(macc)
                        for mp in range(2):
                            st = scores(c, mp)
                            s_scr[j % 2, mp, c] = st.reshape(ATT_KC, ATT_BLK)
                            macc[mp] = jnp.maximum(macc[mp], jnp.max(st, axis=0))
                        macc = tuple(macc)
                    if do_e:
                        jb = j - d_exp
                        lacc = tuple(
                            exps(c, jb % 2, mp, s_scr[jb % 2, mp, c].reshape(sub, 8, ATT_BLK), m8[jb][mp], lacc[mp])
                            for mp in range(2))
                elif do_s:
                    lacc = tuple(exps(c, j % 2, mp, scores(c, mp), bound8, lacc[mp]) for mp in range(2))
                if do_v:
                    values(c, (j - d_val) % 2, r8[j - d_val])
                return macc, lacc

            macc, lacc = lax.fori_loop(0, nkc, body, ((neg, neg), (zero8, zero8)),
                                       unroll=ATT_UNROLL if exact_max else nkc)
            if do_v:
                finish(j - d_val, inv_l1.pop(j - d_val))
                r8.pop(j - d_val)
            if do_e:
                l1 = jnp.sum(lacc[0], axis=0, keepdims=True)
                l2 = jnp.sum(lacc[1], axis=0, keepdims=True)
                r8[j - d_exp] = jnp.broadcast_to(lam * l1 / l2, (8, ATT_BLK))
                inv_l1[j - d_exp] = 1.0 / l1
                m8.pop(j - d_exp, None)
            if exact_max and do_s:
                m8[j] = tuple(jnp.broadcast_to(jnp.max(a, axis=0, keepdims=True), (8, ATT_BLK)) for a in macc)

    use_bound = bound <= 0.5 * (126.0 - ATT_EXP2_HEADROOM)

    @pl.when(use_bound)
    def _():
        pipeline(False)

    @pl.when(jnp.logical_not(use_bound))
    def _():
        pipeline(True)


def _diff_attn(bound, qa, k_all, vt_all, za, lam, head_g):
    b, t, _ = qa.shape
    n_keys = k_all.shape[1]
    nkc = n_keys // ATT_KC
    assert t % ATT_BLK == 0 and n_keys % ATT_KC == 0 and nkc % ATT_UNROLL == 0
    qmap = lambda bi, h: (bi, 0, h)
    const2 = lambda bi, h: (0, 0)
    return pl.pallas_call(
        _attn_kernel,
        grid=(b, DA_HEADS),
        in_specs=[pl.BlockSpec(memory_space=pltpu.SMEM),
                  pl.BlockSpec((1, t, DA_HEAD_W), qmap),
                  pl.BlockSpec((1, n_keys, DA_HEAD_W), qmap),
                  pl.BlockSpec((1, nkc, DA_HEAD_W, ATT_KC), lambda bi, h: (bi, 0, h, 0)),
                  pl.BlockSpec((1, t, DA_HEAD_W), qmap),
                  pl.BlockSpec((1, LANES), const2),
                  pl.BlockSpec((1, DA_HEAD_W), const2)],
        out_specs=pl.BlockSpec((1, t, DA_HEAD_W), qmap),
        out_shape=jax.ShapeDtypeStruct((b, t, DA_WIDTH), BF16),
        scratch_shapes=[pltpu.VMEM((2, 2, nkc, ATT_KC, ATT_BLK), F32),
                        pltpu.VMEM((2, 2, nkc, ATT_KC, ATT_BLK), F32),
                        pltpu.VMEM((DA_HEAD_W, ATT_BLK), F32),
                        pltpu.VMEM((2, ATT_BLK, DA_HEAD_W), BF16)],
        compiler_params=pltpu.CompilerParams(dimension_semantics=("parallel", "arbitrary"),
                                             vmem_limit_bytes=VMEM_LIMIT),
        name="diff_attn",
    )(bound, qa, k_all, vt_all, za, lam, head_g)


def _conv_silu(x, w, b):
    t = x.shape[0]
    row = lax.broadcasted_iota(jnp.int32, x.shape, 0)
    prev = jnp.where(row == 0, 0.0, pltpu.roll(x, 1, 0))
    nxt = jnp.where(row == t - 1, 0.0, pltpu.roll(x, t - 1, 0))
    y = b + w[0:1, :] * prev
    y = y + w[1:2, :] * x
    y = y + w[2:3, :] * nxt
    return _silu(y)


def _rows_to_cols(rows, eye):
    nt = (((1,), (1,)), ((), ()))
    out = None
    for part in _split3(rows):
        term = lax.dot_general(eye, part, nt, preferred_element_type=F32)
        out = term if out is None else out + term
    return out


def _cumsum_rows(rows, tri):
    out = None
    for part in _split3(rows):
        term = jnp.dot(part, tri, preferred_element_type=F32)
        out = term if out is None else out + term
    return out


def _cummax_rows(rows, reverse):
    n = rows.shape[1]
    lane = lax.broadcasted_iota(jnp.int32, rows.shape, 1)
    sh = 1
    while sh < n:
        if reverse:
            shifted = jnp.where(lane < n - sh, pltpu.roll(rows, n - sh, 1), -jnp.inf)
        else:
            shifted = jnp.where(lane >= sh, pltpu.roll(rows, sh, 1), -jnp.inf)
        rows = jnp.maximum(rows, shifted)
        sh *= 2
    return rows


def _mlstm_kernel(xl_ref, xc_ref, vt_ref, gl_ref, gc_ref, zb_ref, ob_ref,
                  cw_ref, cb_ref, wqt_ref, wk_ref, hg_ref, sk_ref, o_ref,
                  xcv_s, qt_s, k_s, vat_s, hf_s, hb_s, upd_s, stb_s, r1_s, bcol_s):
    hp = pl.program_id(1)
    n_lat = xl_ref.shape[1]
    nchunk = n_lat // ML_L
    L = ML_L
    nt = (((1,), (1,)), ((), ()))

    r_i = lax.broadcasted_iota(jnp.int32, (L, L), 0)
    c_i = lax.broadcasted_iota(jnp.int32, (L, L), 1)
    eye = jnp.where(r_i == c_i, 1.0, 0.0).astype(BF16)
    tris = (jnp.where(r_i <= c_i, 1.0, 0.0).astype(BF16), jnp.where(r_i >= c_i, 1.0, 0.0).astype(BF16))
    visible = (r_i <= c_i, r_i >= c_i)
    row = lax.broadcasted_iota(jnp.int32, (VA_ROWS - ML_HD, L), 0)
    ones_row = jnp.where(row == 0, 1.0, 0.0).astype(BF16)
    lane8 = lax.broadcasted_iota(jnp.int32, (8, L), 1)
    zero11 = jnp.zeros((1, 1), F32)

    gates = {}
    for hh in range(ML_HPP):
        head = hp * ML_HPP + hh
        cols_h = slice(hh * ML_HD, (hh + 1) * ML_HD)
        wqt = wqt_ref[hh]
        wk = wk_ref[hh]

        def prepare(x_ref, slot0, hh=hh, cols_h=cols_h, wqt=wqt, wk=wk):
            n = x_ref.shape[1]
            xc = _conv_silu(x_ref[0, :, cols_h].astype(F32), cw_ref[:, cols_h], cb_ref[:, cols_h])
            xb = xc.astype(BF16)
            qt = lax.dot_general(wqt, xb, nt, preferred_element_type=F32).astype(BF16)
            for j in range(n // L):
                qt_s[hh, slot0 + j] = qt[:, j * L:(j + 1) * L]
            k = jnp.dot(xb, wk, preferred_element_type=F32) * (ML_HD ** -0.5)
            k_s[hh, slot0 * L:slot0 * L + n, :] = k.astype(BF16)
            return xc

        prepare(xc_ref, nchunk)
        xcv_s[hh] = prepare(xl_ref, 0)
        for j in range(nchunk + 1):
            vat_s[hh, j, 0:ML_HD, :] = vt_ref[0, j, cols_h, :]
            vat_s[hh, j, ML_HD:VA_ROWS, :] = ones_row

        for d in range(2):
            per_slot = {}
            for g_ref, r0 in ((gl_ref, 0), (gc_ref, nchunk)):
                ig = g_ref[0, (2 * d) * ML_HEADS + head] * LOG2E
                lf = g_ref[0, (2 * d + 1) * ML_HEADS + head] * LOG2E
                n = ig.shape[0]
                if n == 1:
                    ig = jnp.broadcast_to(ig, (8, L))
                    lf = jnp.broadcast_to(lf, (8, L))
                a = _cumsum_rows(lf, tris[d])
                bb = ig - a
                pick = (lane8 == L - 1) if d == 0 else (lane8 == 0)
                f_tot = jnp.sum(jnp.where(pick, a, 0.0), axis=-1, keepdims=True)
                b_max = jnp.max(bb, axis=-1, keepdims=True)
                cmax = _cummax_rows(bb, d == 1)
                cols = _rows_to_cols(bb, eye)
                for j in range(n):
                    bcol_s[hh * 2 + d, r0 + j] = jnp.broadcast_to(cols[:, j:j + 1], (L, ML_HD))
                    per_slot[r0 + j] = (a[j:j + 1], cmax[j:j + 1], f_tot[j:j + 1], b_max[j:j + 1])
            m = zero11
            order = [nchunk] + (list(range(nchunk)) if d == 0 else list(range(nchunk - 1, -1, -1)))
            for slot in order:
                a_row, cmax_row, f_tot, b_max = per_slot[slot]
                m_new = jnp.maximum(f_tot + m, f_tot + b_max)
                gates[hh, d, slot] = (a_row, cmax_row, f_tot, m, m_new)
                m = m_new

    latent = [(hh, d, c) for hh in range(ML_HPP) for d in range(2) for c in range(nchunk)]
    every = [(hh, d, nchunk) for hh in range(ML_HPP) for d in range(2)] + latent

    def scores(item):
        hh, _, c = item
        return jnp.dot(k_s[hh, c * L:(c + 1) * L, :], qt_s[hh, c], preferred_element_type=F32)

    def decayed(item, st):
        hh, d, c = item
        _, cmax_row, _, m_old, _ = gates[item]
        mm = jnp.maximum(cmax_row, m_old)
        bcol = bcol_s[hh * 2 + d, c]
        logd = jnp.where(visible[d], jnp.concatenate([bcol] * (L // ML_HD), axis=1) - mm, -jnp.inf)
        return (st * jnp.exp2(logd)).astype(BF16)

    def increment(item):
        hh, d, slot = item
        _, _, f_tot, _, m_new = gates[item]
        kc = k_s[hh, slot * L:(slot + 1) * L, :]
        kw = (kc.astype(F32) * jnp.exp2(bcol_s[hh * 2 + d, slot] + (f_tot - m_new))).astype(BF16)
        upd_s[hh * 2 + d, slot] = jnp.dot(vat_s[hh, slot], kw, preferred_element_type=F32)

    n_lat = len(latent)
    s_val, p_val = {}, {}
    for t in range(n_lat + 2):
        if t < n_lat:
            s_val[t] = scores(latent[t])
        if 0 <= t - 1 < n_lat:
            p_val[t - 1] = decayed(latent[t - 1], s_val.pop(t - 1))
        if 0 <= t - 2 < n_lat:
            hh, d, c = latent[t - 2]
            r1_s[hh * 2 + d, c] = jnp.dot(vat_s[hh, c], p_val.pop(t - 2), preferred_element_type=F32)
        if t < len(every):
            increment(every[t])
    for t in range(n_lat + 2, len(every)):
        increment(every[t])

    for hh in range(ML_HPP):
        for d in range(2):
            sd = hh * 2 + d
            st = upd_s[sd, nchunk]
            for c in (range(nchunk) if d == 0 else range(nchunk - 1, -1, -1)):
                _, _, f_tot, m_old, m_new = gates[hh, d, c]
                stb_s[sd, c] = st.astype(BF16)
                st = jnp.exp2(f_tot + m_old - m_new) * st + upd_s[sd, c]

    def carried(item):
        hh, d, c = item
        return jnp.dot(stb_s[hh * 2 + d, c], qt_s[hh, c], preferred_element_type=F32)

    def emit(item, r2):
        hh, d, c = item
        a_row, cmax_row, _, m_old, _ = gates[item]
        mm = jnp.maximum(cmax_row, m_old)
        tot = r1_s[hh * 2 + d, c] + jnp.exp2(m_old - mm) * r2
        den = tot[ML_HD:ML_HD + 1, :]
        floor = jnp.exp2(-(a_row + mm))
        (hf_s if d == 0 else hb_s)[hh, c] = tot[0:ML_HD, :] * (1.0 / jnp.maximum(jnp.abs(den), floor))

    r2_val = {}
    for t in range(n_lat + 1):
        if t < n_lat:
            r2_val[t] = carried(latent[t])
        if t >= 1:
            emit(latent[t - 1], r2_val.pop(t - 1))


    for hh in range(ML_HPP):
        cols_h = slice(hh * ML_HD, (hh + 1) * ML_HD)
        for c in range(nchunk):
            ht = hf_s[hh, c] + hb_s[hh, c]
            ms = jnp.mean(ht * ht, axis=0, keepdims=True)
            hn = (ht * lax.rsqrt(ms + EPS)).T * hg_ref[:, cols_h]
            rows = slice(c * L, (c + 1) * L)
            y = ((ob_ref[0, rows, cols_h].astype(F32) * hn + sk_ref[:, cols_h] * xcv_s[hh, rows, :])
                 * zb_ref[0, rows, cols_h].astype(F32))
            o_ref[0, rows, cols_h] = y.astype(BF16)


def _mlstm(xm_l, xm_c, vt_all, gt_l, gt_c, zb, ob, conv_w, conv_b, wqt, wk, head_g, skip):
    b, t, _ = xm_l.shape
    n_ctx = xm_c.shape[1]
    nchunk = t // ML_L
    nslot = nchunk + 1
    assert t % (8 * ML_L) == 0 and n_ctx == ML_L and vt_all.shape[1] == nslot
    gl = gt_l.reshape(b, ML_GATES, nchunk, ML_L)
    gc = gt_c.reshape(b, ML_GATES, 1, ML_L)
    pw = ML_HPP * ML_HD
    vt_blk0 = DA_WIDTH // pw
    tokmap = lambda bi, h: (bi, 0, h)
    gmap = lambda bi, h: (bi, 0, 0, 0)
    hvec = lambda bi, h: (0, h)
    wmap = lambda bi, h: (h, 0, 0)
    return pl.pallas_call(
        _mlstm_kernel,
        grid=(b, ML_HEADS // ML_HPP),
        in_specs=[pl.BlockSpec((1, t, pw), tokmap),
                  pl.BlockSpec((1, n_ctx, pw), tokmap),
                  pl.BlockSpec((1, nslot, pw, ML_L), lambda bi, h: (bi, 0, vt_blk0 + h, 0)),
                  pl.BlockSpec((1, ML_GATES, nchunk, ML_L), gmap),
                  pl.BlockSpec((1, ML_GATES, 1, ML_L), gmap),
                  pl.BlockSpec((1, t, pw), tokmap),
                  pl.BlockSpec((1, t, pw), tokmap),
                  pl.BlockSpec((conv_w.shape[0], pw), hvec),
                  pl.BlockSpec((1, pw), hvec),
                  pl.BlockSpec((ML_HPP, ML_HD, ML_HD), wmap),
                  pl.BlockSpec((ML_HPP, ML_HD, ML_HD), wmap),
                  pl.BlockSpec((1, pw), hvec),
                  pl.BlockSpec((1, pw), hvec)],
        out_specs=pl.BlockSpec((1, t, pw), tokmap),
        out_shape=jax.ShapeDtypeStruct((b, t, ML_WIDTH), BF16),
        scratch_shapes=[pltpu.VMEM((ML_HPP, t, ML_HD), F32),
                        pltpu.VMEM((ML_HPP, nslot, ML_HD, ML_L), BF16),
                        pltpu.VMEM((ML_HPP, nslot * ML_L, ML_HD), BF16),
                        pltpu.VMEM((ML_HPP, nslot, VA_ROWS, ML_L), BF16),
                        pltpu.VMEM((ML_HPP, nchunk, ML_HD, ML_L), F32),
                        pltpu.VMEM((ML_HPP, nchunk, ML_HD, ML_L), F32),
                        pltpu.VMEM((ML_HPP * 2, nslot, VA_ROWS, ML_HD), F32),
                        pltpu.VMEM((ML_HPP * 2, nchunk, VA_ROWS, ML_HD), BF16),
                        pltpu.VMEM((ML_HPP * 2, nchunk, VA_ROWS, ML_L), F32),
                        pltpu.VMEM((ML_HPP * 2, nslot, ML_L, ML_HD), F32)],
        compiler_params=pltpu.CompilerParams(dimension_semantics=("parallel", "arbitrary"),
                                             vmem_limit_bytes=VMEM_LIMIT),
        name="mlstm",
    )(xm_l, xm_c, vt_all, gl, gc, zb, ob, conv_w, conv_b, wqt, wk, head_g, skip)


def _outproj_kernel(x_ref, ya_ref, yb_ref, gg_ref, gt_ref, woa_ref, wob_ref, wo_ref, o_ref):
    d = x_ref.shape[2]
    a = jnp.dot(ya_ref[0], woa_ref[...], preferred_element_type=F32)
    b = jnp.dot(yb_ref[0], wob_ref[...], preferred_element_type=F32)
    y = gg_ref[0, :, 0:d].astype(F32) * a + gg_ref[0, :, d:2 * d].astype(F32) * b
    o = jnp.dot(y.astype(BF16), wo_ref[...], preferred_element_type=F32)
    o_ref[0] = x_ref[0] + gt_ref[0] * o


def _out_proj(x, ya, yb, gg, gt, woa, wob, wo):
    b, t, d = x.shape
    tm = min(OUT_TILE, t)
    tok = lambda width: pl.BlockSpec((1, tm, width), lambda bi, i: (bi, i, 0))
    const2 = lambda bi, i: (0, 0)
    return pl.pallas_call(
        _outproj_kernel,
        grid=(b, t // tm),
        in_specs=[tok(d), tok(DA_WIDTH), tok(ML_WIDTH), tok(2 * d),
                  pl.BlockSpec((1, 1, d), lambda bi, i: (bi, 0, 0)),
                  pl.BlockSpec(woa.shape, const2), pl.BlockSpec(wob.shape, const2),
                  pl.BlockSpec(wo.shape, const2)],
        out_specs=tok(d),
        out_shape=jax.ShapeDtypeStruct((b, t, d), x.dtype),
        compiler_params=pltpu.CompilerParams(dimension_semantics=("parallel", "parallel"),
                                             vmem_limit_bytes=VMEM_LIMIT),
        name="out_proj",
    )(x, ya, yb, gg, gt, woa, wob, wo)


def _rope_tables(n_tokens):
    rows = n_tokens // GRID_W
    row_id = jnp.repeat(jnp.arange(rows, dtype=F32), GRID_W)
    col_id = jnp.tile(jnp.arange(GRID_W, dtype=F32), rows)
    n_freq = DA_HD // 4
    inv_freq = ROPE_THETA ** (-jnp.arange(n_freq, dtype=F32) / n_freq)
    ang_r = row_id[:, None] * inv_freq
    ang_c = col_id[:, None] * inv_freq
    cos = jnp.concatenate([jnp.cos(ang_r)] * 2 + [jnp.cos(ang_c)] * 2, axis=-1)
    sin = jnp.concatenate([-jnp.sin(ang_r), jnp.sin(ang_r), -jnp.sin(ang_c), jnp.sin(ang_c)], axis=-1)
    return jnp.tile(cos, (1, LANES // DA_HD)), jnp.tile(sin, (1, LANES // DA_HD))


def kernel(x, c, ctx, c_ctx, norm_w, w_mod, b_mod, w_in, b_if, da_q_norm, da_k_norm, da_lambda_q1, da_lambda_k1, da_lambda_q2, da_lambda_k2, da_head_norm, w_out_a, ml_conv_w, ml_conv_b, ml_wq, ml_wk, ml_head_norm, ml_skip, w_out_b, w_o):
    assert w_mod.shape[0] == 1, "single-layer block"
    b, t, d = x.shape

    rows = -(-(b + 1) // 8) * 8
    cc = jnp.zeros((rows, d), F32).at[:b].set(c).at[b].set(c_ctx)
    mod, lam, bound = _modulation(cc, w_mod[0], b_mod, da_lambda_q1, da_lambda_k1, da_lambda_q2, da_lambda_k2,
                                  da_q_norm, da_k_norm)
    sh_l, sc_l, gt_l = (mod[:b, i * d:(i + 1) * d].reshape(b, 1, d) for i in range(3))
    sh_c, sc_c = (mod[b:b + 1, i * d:(i + 1) * d].reshape(1, 1, d) for i in range(2))

    w_main = w_in[0].astype(BF16)
    w_tail = w_main[:, N_MAIN + ML_GATES:]
    w_vt = jnp.concatenate([w_main[:, COL_VA:COL_ZA], w_main[:, COL_VM:COL_ZB],
                            w_main[:, N_MAIN:N_MAIN + ML_GATES]], axis=1).T
    bif = b_if.reshape(ML_GATES, 1)

    grp = np.arange(DA_WIDTH) // DA_HD
    bd = jnp.asarray((grp[:, None] == grp[None, :]).astype(np.float32) / DA_HD, dtype=BF16)
    gq = jnp.tile(da_q_norm, (1, DA_WIDTH // DA_HD))
    gk = jnp.tile(da_k_norm, (1, DA_WIDTH // DA_HD))
    cos, sin = _rope_tables(t)

    n_keys = t + ctx.shape[1]
    qa, k_all, vt_all, za, xm_l, zb, ob, gg, gtl = _in_proj(
        x, sc_l, sh_l, norm_w, w_main, w_vt, bif, (w_tail, bd, gq, gk, cos, sin), latent=True, n_keys=n_keys, key_off=0)
    k_all, vt_all, xm_c, gtc = _in_proj(
        ctx, sc_c, sh_c, norm_w, w_main, w_vt, bif, (bd, gk, k_all, vt_all), latent=False, n_keys=n_keys, key_off=t)

    ya = _diff_attn(bound[0, :1], qa, k_all, vt_all, za, lam, da_head_norm)
    yb = _mlstm(xm_l, xm_c, vt_all, gtl, gtc, zb, ob, ml_conv_w[0], ml_conv_b,
                jnp.swapaxes(ml_wq[0], 1, 2).astype(BF16), ml_wk[0].astype(BF16), ml_head_norm, ml_skip)
    return _out_proj(x, ya, yb, gg, gt_l, w_out_a[0].astype(BF16), w_out_b[0].astype(BF16),
                     w_o[0].astype(BF16))
```

```python
import functools

import jax
import jax.numpy as jnp
import numpy as np
from jax import lax
from jax.experimental import pallas as pl
from jax.experimental.pallas import tpu as pltpu

F32 = jnp.float32
BF16 = jnp.bfloat16

EPS = 1e-6
ROPE_THETA = 10000.0
GRID_W = 64
LAM_INIT = 0.8 - 0.6 * 1.0

DA_HEADS = 4
DA_HD = 64
DA_HEAD_W = 2 * DA_HD
DA_WIDTH = DA_HEADS * DA_HEAD_W

ML_HEADS = 4
ML_HD = 128
ML_WIDTH = ML_HEADS * ML_HD
ML_GATES = 4 * ML_HEADS

COL_QA, COL_KA, COL_VA, COL_ZA = (i * DA_WIDTH for i in range(4))
COL_XM, COL_VM, COL_ZB, COL_OB = (4 * DA_WIDTH + i * ML_WIDTH for i in range(4))
N_MAIN = 4 * DA_WIDTH + 4 * ML_WIDTH
ML_L = 256
VA_ROWS = ML_HD + 16
LOG2E = 1.4426950408889634
ML_HPP = 2

LANES = 128
VMEM_LIMIT = 56 * 1024 * 1024

IN_TILE = 512
OUT_TILE = 1024
ATT_BLK = 256
ATT_KC = 256
ATT_UNROLL = 3
ATT_EXP2_HEADROOM = 30.0
SCORE_BOUND_MARGIN = 1.02
Q_SCALE = DA_HD ** -0.5 * LOG2E


def _sigmoid(x):
    return 0.5 * jnp.tanh(0.5 * x) + 0.5


def _silu(x):
    h = 0.5 * x
    return h * jnp.tanh(h) + h


def _split3(x):
    hi = x.astype(BF16)
    r1 = x - hi.astype(F32)
    mid = r1.astype(BF16)
    lo = (r1 - mid.astype(F32)).astype(BF16)
    return hi, mid, lo


def _mod_kernel(cc_ref, w_ref, b_ref, lq1_ref, lk1_ref, lq2_ref, lk2_ref, gq_ref, gk_ref,
                mod_ref, lam_ref, bound_ref):
    a = _silu(cc_ref[...]).astype(BF16)
    mod_ref[...] = jnp.dot(a, w_ref[...].astype(BF16), preferred_element_type=F32) + b_ref[...]
    s1 = jnp.sum(lq1_ref[...] * lk1_ref[...], axis=-1, keepdims=True)
    s2 = jnp.sum(lq2_ref[...] * lk2_ref[...], axis=-1, keepdims=True)
    lam = jnp.exp(s1) - jnp.exp(s2) + LAM_INIT
    lam_ref[...] = jnp.broadcast_to(lam, lam_ref.shape)
    bq = jnp.max(jnp.abs(gq_ref[...]), axis=-1, keepdims=True)
    bk = jnp.max(jnp.abs(gk_ref[...]), axis=-1, keepdims=True)
    bound_ref[...] = jnp.broadcast_to(SCORE_BOUND_MARGIN * Q_SCALE * DA_HD * bq * bk, bound_ref.shape)


def _modulation(cc, w_mod, b_mod, lq1, lk1, lq2, lk2, gq, gk):
    rows, d = cc.shape
    n3 = w_mod.shape[1]
    vec = pl.BlockSpec((1, DA_HD), lambda j: (0, 0))
    return pl.pallas_call(
        _mod_kernel,
        grid=(n3 // d,),
        in_specs=[pl.BlockSpec((rows, d), lambda j: (0, 0)),
                  pl.BlockSpec((d, d), lambda j: (0, j)),
                  pl.BlockSpec((1, d), lambda j: (0, j)),
                  vec, vec, vec, vec, vec, vec],
        out_specs=[pl.BlockSpec((rows, d), lambda j: (0, j)),
                   pl.BlockSpec((1, LANES), lambda j: (0, 0)),
                   pl.BlockSpec((1, LANES), lambda j: (0, 0))],
        out_shape=[jax.ShapeDtypeStruct((rows, n3), F32),
                   jax.ShapeDtypeStruct((1, LANES), F32),
                   jax.ShapeDtypeStruct((1, LANES), F32)],
        compiler_params=pltpu.CompilerParams(dimension_semantics=("arbitrary",),
                                             vmem_limit_bytes=VMEM_LIMIT),
        name="modulation",
    )(cc, w_mod, b_mod, lq1, lk1, lq2, lk2, gq, gk)


def _group_rms(acc, bd_ref, g_ref):
    ms = jnp.dot((acc * acc).astype(BF16), bd_ref[...], preferred_element_type=F32)
    return acc * lax.rsqrt(ms + EPS) * g_ref[...]


def _rope(x, cos, sin_signed, first_half):
    outs = []
    for c in range(x.shape[1] // LANES):
        xs = x[:, c * LANES:(c + 1) * LANES]
        partner = jnp.where(first_half, pltpu.roll(xs, LANES - 16, 1), pltpu.roll(xs, 16, 1))
        outs.append(xs * cos + partner * sin_signed)
    return jnp.concatenate(outs, axis=1)


def _inproj_kernel(x_ref, sc_ref, sh_ref, nw_ref, w_ref, wt_ref, wvt_ref, bif_ref, *rest, latent):
    if latent:
        (bd_ref, gq_ref, gk_ref, cos_ref, sin_ref,
         qa_ref, ka_ref, vt_ref, za_ref, xm_ref, zb_ref, ob_ref, gg_ref, gt_ref) = rest
    else:
        bd_ref, gk_ref, _, _, ka_ref, vt_ref, xm_ref, gt_ref = rest

    xf = x_ref[0]
    ms = jnp.mean(xf * xf, axis=-1, keepdims=True)
    y = xf * lax.rsqrt(ms + EPS) * nw_ref[...]
    h = (y * (1.0 + sc_ref[0]) + sh_ref[0]).astype(BF16)

    def proj(ref, c0, width):
        return jnp.dot(h, ref[:, c0:c0 + width], preferred_element_type=F32)

    vt = lax.dot_general(wvt_ref[...], h, (((1,), (1,)), ((), ())), preferred_element_type=F32).astype(BF16)
    for j in range(vt_ref.shape[1]):
        vt_ref[0, j] = vt[:, j * ATT_KC:(j + 1) * ATT_KC]

    if latent:
        tm = xf.shape[0]
        lane = lax.broadcasted_iota(jnp.int32, (tm, LANES), 1)
        first_half = (lane % 32) < 16
        cos = cos_ref[...]
        sin = sin_ref[...]
        q = _rope(_group_rms(proj(w_ref, COL_QA, DA_WIDTH), bd_ref, gq_ref), cos, sin, first_half)
        qa_ref[0] = (q * Q_SCALE).astype(BF16)
        k = _rope(_group_rms(proj(w_ref, COL_KA, DA_WIDTH), bd_ref, gk_ref), cos, sin, first_half)
        ka_ref[0] = k.astype(BF16)
        za_ref[0] = _silu(proj(w_ref, COL_ZA, DA_WIDTH)).astype(BF16)
        zb_ref[0] = _silu(proj(w_ref, COL_ZB, ML_WIDTH)).astype(BF16)
        ob_ref[0] = _sigmoid(proj(w_ref, COL_OB, ML_WIDTH)).astype(BF16)
        for j in range(gg_ref.shape[2] // 512):
            gg_ref[0, :, j * 512:(j + 1) * 512] = _sigmoid(proj(wt_ref, j * 512, 512)).astype(BF16)
    else:
        ka_ref[0] = _group_rms(proj(w_ref, COL_KA, DA_WIDTH), bd_ref, gk_ref).astype(BF16)

    g = proj(wt_ref, wt_ref.shape[1] - LANES, LANES).T[:ML_GATES] + bif_ref[...]
    row = lax.broadcasted_iota(jnp.int32, g.shape, 0)
    logsig = jnp.minimum(g, 0.0) - jnp.log(1.0 + jnp.exp(-jnp.abs(g)))
    gt_ref[0] = jnp.where((row // ML_HEADS) % 2 == 1, logsig, g)

    xm_ref[0] = proj(w_ref, COL_XM, ML_WIDTH).astype(BF16)


def _in_proj(x, sc, sh, norm_w, w, w_tail, wvt, bif, extra, *, latent, n_keys, key_off):
    b, t, d = x.shape
    tm = min(IN_TILE, t)
    assert key_off % tm == 0
    kblk = key_off // tm
    grid = (b, t // tm)
    const2 = lambda bi, i: (0, 0)
    mod_map = (lambda bi, i: (bi, 0, 0)) if latent else (lambda bi, i: (0, 0, 0))
    tok = lambda width, dtype: (pl.BlockSpec((1, tm, width), lambda bi, i: (bi, i, 0)),
                                jax.ShapeDtypeStruct((b, t, width), dtype))
    keys = (pl.BlockSpec((1, tm, DA_WIDTH), lambda bi, i: (bi, kblk + i, 0)),
            jax.ShapeDtypeStruct((b, n_keys, DA_WIDTH), BF16))
    cpt = tm // ATT_KC
    vals = (pl.BlockSpec((1, cpt, wvt.shape[0], ATT_KC), lambda bi, i: (bi, kblk + i, 0, 0)),
            jax.ShapeDtypeStruct((b, n_keys // ATT_KC, wvt.shape[0], ATT_KC), BF16))
    w_tail_spec = (pl.BlockSpec(w_tail.shape, const2) if latent else
                   pl.BlockSpec((d, LANES), lambda bi, i: (0, w_tail.shape[1] // LANES - 1)))
    in_specs = [pl.BlockSpec((1, tm, d), lambda bi, i: (bi, i, 0)),
                pl.BlockSpec((1, 1, d), mod_map),
                pl.BlockSpec((1, 1, d), mod_map),
                pl.BlockSpec((1, d), const2),
                pl.BlockSpec((d, N_MAIN), const2),
                w_tail_spec,
                pl.BlockSpec(wvt.shape, const2),
                pl.BlockSpec(bif.shape, const2)]
    if latent:
        bd, gq, gk, cos, sin = extra
        in_specs += [pl.BlockSpec(bd.shape, const2), pl.BlockSpec(gq.shape, const2),
                     pl.BlockSpec(gk.shape, const2),
                     pl.BlockSpec((tm, LANES), lambda bi, i: (i, 0)),
                     pl.BlockSpec((tm, LANES), lambda bi, i: (i, 0))]
        outs = ([tok(DA_WIDTH, BF16), keys, vals, tok(DA_WIDTH, BF16)] + [tok(ML_WIDTH, BF16)] * 3
                + [tok(2 * d, BF16)])
        aliases = {}
    else:
        bd, gk, _, _ = extra
        in_specs += [pl.BlockSpec(bd.shape, const2), pl.BlockSpec(gk.shape, const2),
                     pl.BlockSpec(memory_space=pl.ANY), pl.BlockSpec(memory_space=pl.ANY)]
        outs = [keys, vals, tok(ML_WIDTH, BF16)]
        aliases = {len(in_specs) - 2: 0, len(in_specs) - 1: 1}
    outs.append((pl.BlockSpec((1, ML_GATES, tm), lambda bi, i: (bi, 0, i)),
                 jax.ShapeDtypeStruct((b, ML_GATES, t), F32)))
    return pl.pallas_call(
        functools.partial(_inproj_kernel, latent=latent),
        grid=grid,
        in_specs=in_specs,
        out_specs=[o[0] for o in outs],
        out_shape=[o[1] for o in outs],
        input_output_aliases=aliases,
        compiler_params=pltpu.CompilerParams(dimension_semantics=("parallel", "parallel"),
                                             vmem_limit_bytes=VMEM_LIMIT),
        name="in_proj_latent" if latent else "in_proj_context",
    )(x, sc, sh, norm_w, w, w_tail, wvt, bif, *extra)


def _attn_kernel(bound_ref, q_ref, k_ref, vt_ref, z_ref, lam_ref, hg_ref, o_ref, s_scr, e_scr, acc_scr, qm_scr):
    nblk = q_ref.shape[1] // ATT_BLK
    nkc = k_ref.shape[1] // ATT_KC
    sub = ATT_KC // 8
    lam = lam_ref[:, 0:1]
    nt = (((1,), (1,)), ((), ()))
    neg = jnp.full((8, ATT_BLK), -jnp.inf, F32)
    zero8 = jnp.zeros((8, ATT_BLK), F32)
    bound = bound_ref[0]

    def mask_queries(j):
        q = q_ref[0, j * ATT_BLK:(j + 1) * ATT_BLK, :]
        lane = lax.broadcasted_iota(jnp.int32, q.shape, 1)
        zero = jnp.zeros_like(q)
        qm_scr[0] = jnp.where(lane < DA_HD, q, zero)
        qm_scr[1] = jnp.where(lane >= DA_HD, q, zero)

    def scores(c, mp):
        kc = k_ref[0, pl.ds(pl.multiple_of(c * ATT_KC, ATT_KC), ATT_KC), :]
        return lax.dot_general(kc, qm_scr[mp], nt, preferred_element_type=F32).reshape(sub, 8, ATT_BLK)

    def exps(c, par, mp, st, stab8, lacc):
        e = jnp.exp2(st - stab8[None])
        e_scr[par, mp, c] = e.reshape(ATT_KC, ATT_BLK)
        return lacc + jnp.sum(e, axis=0)

    def values(c, par, r8):
        e1 = e_scr[par, 0, c].reshape(sub, 8, ATT_BLK)
        e2 = e_scr[par, 1, c].reshape(sub, 8, ATT_BLK)
        pt = (e1 - e2 * r8[None]).reshape(ATT_KC, ATT_BLK).astype(BF16)
        acc_scr[...] += jnp.dot(vt_ref[0, c], pt, preferred_element_type=F32)

    def finish(j, inv_l1):
        o = (acc_scr[...] * inv_l1).T
        ms = jnp.mean(o * o, axis=-1, keepdims=True)
        on = o * lax.rsqrt(ms + EPS) * hg_ref[...]
        rows = slice(j * ATT_BLK, (j + 1) * ATT_BLK)
        o_ref[0, rows, :] = (on * (1.0 - LAM_INIT) * z_ref[0, rows, :].astype(F32)).astype(BF16)

    def pipeline(exact_max):
        d_exp = 1 if exact_max else 0
        d_val = d_exp + 1
        bound8 = jnp.full((8, ATT_BLK), bound, F32)
        m8, r8, inv_l1 = {}, {}, {}
        for j in range(nblk + d_val):
            do_s, do_e, do_v = j < nblk, d_exp <= j < nblk + d_exp, d_val <= j
            if do_s:
                mask_queries(j)
            if do_v:
                acc_scr[...] = jnp.zeros_like(acc_scr)

            def body(c, carry, j=j, do_s=do_s, do_e=do_e, do_v=do_v):
                macc, lacc = carry
                if exact_max:
                    if do_s:
                        macc = list(macc)
                        for mp in range(2):
                            st = scores(c, mp)
                            s_scr[j % 2, mp, c] = st.reshape(ATT_KC, ATT_BLK)
                            macc[mp] = jnp.maximum(macc[mp], jnp.max(st, axis=0))
                        macc = tuple(macc)
                    if do_e:
                        jb = j - d_exp
                        lacc = tuple(
                            exps(c, jb % 2, mp, s_scr[jb % 2, mp, c].reshape(sub, 8, ATT_BLK), m8[jb][mp], lacc[mp])
                            for mp in range(2))
                elif do_s:
                    lacc = tuple(exps(c, j % 2, mp, scores(c, mp), bound8, lacc[mp]) for mp in range(2))
                if do_v:
                    values(c, (j - d_val) % 2, r8[j - d_val])
                return macc, lacc

            macc, lacc = lax.fori_loop(0, nkc, body, ((neg, neg), (zero8, zero8)),
                                       unroll=ATT_UNROLL if exact_max else nkc)
            if do_v:
                finish(j - d_val, inv_l1.pop(j - d_val))
                r8.pop(j - d_val)
            if do_e:
                l1 = jnp.sum(lacc[0], axis=0, keepdims=True)
                l2 = jnp.sum(lacc[1], axis=0, keepdims=True)
                r8[j - d_exp] = jnp.broadcast_to(lam * l1 / l2, (8, ATT_BLK))
                inv_l1[j - d_exp] = 1.0 / l1
                m8.pop(j - d_exp, None)
            if exact_max and do_s:
                m8[j] = tuple(jnp.broadcast_to(jnp.max(a, axis=0, keepdims=True), (8, ATT_BLK)) for a in macc)

    use_bound = bound <= 0.5 * (126.0 - ATT_EXP2_HEADROOM)

    @pl.when(use_bound)
    def _():
        pipeline(False)

    @pl.when(jnp.logical_not(use_bound))
    def _():
        pipeline(True)


def _diff_attn(bound, qa, k_all, vt_all, za, lam, head_g):
    b, t, _ = qa.shape
    n_keys = k_all.shape[1]
    nkc = n_keys // ATT_KC
    assert t % ATT_BLK == 0 and n_keys % ATT_KC == 0 and nkc % ATT_UNROLL == 0
    qmap = lambda bi, h: (bi, 0, h)
    const2 = lambda bi, h: (0, 0)
    return pl.pallas_call(
        _attn_kernel,
        grid=(b, DA_HEADS),
        in_specs=[pl.BlockSpec(memory_space=pltpu.SMEM),
                  pl.BlockSpec((1, t, DA_HEAD_W), qmap),
                  pl.BlockSpec((1, n_keys, DA_HEAD_W), qmap),
                  pl.BlockSpec((1, nkc, DA_HEAD_W, ATT_KC), lambda bi, h: (bi, 0, h, 0)),
                  pl.BlockSpec((1, t, DA_HEAD_W), qmap),
                  pl.BlockSpec((1, LANES), const2),
                  pl.BlockSpec((1, DA_HEAD_W), const2)],
        out_specs=pl.BlockSpec((1, t, DA_HEAD_W), qmap),
        out_shape=jax.ShapeDtypeStruct((b, t, DA_WIDTH), BF16),
        scratch_shapes=[pltpu.VMEM((2, 2, nkc, ATT_KC, ATT_BLK), F32),
                        pltpu.VMEM((2, 2, nkc, ATT_KC, ATT_BLK), F32),
                        pltpu.VMEM((DA_HEAD_W, ATT_BLK), F32),
                        pltpu.VMEM((2, ATT_BLK, DA_HEAD_W), BF16)],
        compiler_params=pltpu.CompilerParams(dimension_semantics=("parallel", "arbitrary"),
                                             vmem_limit_bytes=VMEM_LIMIT),
        name="diff_attn",
    )(bound, qa, k_all, vt_all, za, lam, head_g)


def _conv_silu(x, w, b, pad_s):
    t = x.shape[0]
    edge = jnp.zeros((8, x.shape[1]), F32)
    pad_s[0:8, :] = edge
    pad_s[8:8 + t, :] = x.astype(F32)
    pad_s[8 + t:16 + t, :] = edge
    y = b + w[0:1, :] * pad_s[7:7 + t, :]
    y = y + w[1:2, :] * pad_s[8:8 + t, :]
    y = y + w[2:3, :] * pad_s[9:9 + t, :]
    return _silu(y)


def _rows_to_cols(rows, eye):
    nt = (((1,), (1,)), ((), ()))
    out = None
    for part in _split3(rows):
        term = lax.dot_general(eye, part, nt, preferred_element_type=F32)
        out = term if out is None else out + term
    return out


def _cumsum_rows(rows, tri):
    out = None
    for part in _split3(rows):
        term = jnp.dot(part, tri, preferred_element_type=F32)
        out = term if out is None else out + term
    return out


def _cummax_rows(rows, reverse):
    n = rows.shape[1]
    lane = lax.broadcasted_iota(jnp.int32, rows.shape, 1)
    sh = 1
    while sh < n:
        if reverse:
            shifted = jnp.where(lane < n - sh, pltpu.roll(rows, n - sh, 1), -jnp.inf)
        else:
            shifted = jnp.where(lane >= sh, pltpu.roll(rows, sh, 1), -jnp.inf)
        rows = jnp.maximum(rows, shifted)
        sh *= 2
    return rows


def _mlstm_kernel(xl_ref, xc_ref, vt_ref, gl_ref, gc_ref, zb_ref, ob_ref,
                  cw_ref, cb_ref, wqt_ref, wk_ref, hg_ref, sk_ref, o_ref,
                  xcv_s, qt_s, k_s, vat_s, hf_s, hb_s, upd_s, stb_s, r1_s, bcol_s, pad_s, xb_s):
    hp = pl.program_id(1)
    n_lat = xl_ref.shape[1]
    nchunk = n_lat // ML_L
    L = ML_L
    nt = (((1,), (1,)), ((), ()))

    r_i = lax.broadcasted_iota(jnp.int32, (L, L), 0)
    c_i = lax.broadcasted_iota(jnp.int32, (L, L), 1)
    eye = jnp.where(r_i == c_i, 1.0, 0.0).astype(BF16)
    tris = (jnp.where(r_i <= c_i, 1.0, 0.0).astype(BF16), jnp.where(r_i >= c_i, 1.0, 0.0).astype(BF16))
    visible = (r_i <= c_i, r_i >= c_i)
    row = lax.broadcasted_iota(jnp.int32, (VA_ROWS - ML_HD, L), 0)
    ones_row = jnp.where(row == 0, 1.0, 0.0).astype(BF16)
    lane8 = lax.broadcasted_iota(jnp.int32, (8, L), 1)
    zero11 = jnp.zeros((1, 1), F32)

    gates = {}
    per_slot = {}

    def prepare(hh, x_ref, slot0):
        cols_h = slice(hh * ML_HD, (hh + 1) * ML_HD)
        n = x_ref.shape[1]
        xc = _conv_silu(x_ref[0, :, cols_h], cw_ref[:, cols_h], cb_ref[:, cols_h], pad_s)
        xb_s[0:n, :] = xc.astype(BF16)
        xb = xb_s[0:n, :]
        qt = lax.dot_general(wqt_ref[hh], xb, nt, preferred_element_type=F32).astype(BF16)
        for j in range(n // L):
            qt_s[hh, slot0 + j] = qt[:, j * L:(j + 1) * L]
        k = jnp.dot(xb, wk_ref[hh], preferred_element_type=F32) * (ML_HD ** -0.5)
        k_s[hh, slot0 * L:slot0 * L + n, :] = k.astype(BF16)
        return xc

    def gate_algebra(hh, d, g_ref, r0):
        head = hp * ML_HPP + hh
        ig = g_ref[0, (2 * d) * ML_HEADS + head] * LOG2E
        lf = g_ref[0, (2 * d + 1) * ML_HEADS + head] * LOG2E
        n = ig.shape[0]
        if n == 1:
            ig = jnp.broadcast_to(ig, (8, L))
            lf = jnp.broadcast_to(lf, (8, L))
        a = _cumsum_rows(lf, tris[d])
        bb = ig - a
        pick = (lane8 == L - 1) if d == 0 else (lane8 == 0)
        f_tot = jnp.sum(jnp.where(pick, a, 0.0), axis=-1, keepdims=True)
        b_max = jnp.max(bb, axis=-1, keepdims=True)
        cmax = _cummax_rows(bb, d == 1)
        cols = _rows_to_cols(bb, eye)
        for j in range(n):
            bcol_s[hh * 2 + d, r0 + j] = jnp.broadcast_to(cols[:, j:j + 1], (L, ML_HD))
            per_slot[hh, d, r0 + j] = (a[j:j + 1], cmax[j:j + 1], f_tot[j:j + 1], b_max[j:j + 1])

    for hh in range(ML_HPP):
        gate_algebra(hh, 0, gl_ref, 0)
        prepare(hh, xc_ref, nchunk)
        gate_algebra(hh, 0, gc_ref, nchunk)
        gate_algebra(hh, 1, gl_ref, 0)
        xcv_s[hh] = prepare(hh, xl_ref, 0)
        gate_algebra(hh, 1, gc_ref, nchunk)
        cols_h = slice(hh * ML_HD, (hh + 1) * ML_HD)
        for j in range(nchunk + 1):
            vat_s[hh, j, 0:ML_HD, :] = vt_ref[0, j, cols_h, :]
            vat_s[hh, j, ML_HD:VA_ROWS, :] = ones_row
        for d in range(2):
            m = zero11
            order = [nchunk] + (list(range(nchunk)) if d == 0 else list(range(nchunk - 1, -1, -1)))
            for slot in order:
                a_row, cmax_row, f_tot, b_max = per_slot[hh, d, slot]
                m_new = jnp.maximum(f_tot + m, f_tot + b_max)
                gates[hh, d, slot] = (a_row, cmax_row, f_tot, m, m_new)
                m = m_new

    latent = [(hh, d, c) for hh in range(ML_HPP) for d in range(2) for c in range(nchunk)]
    every = [(hh, d, nchunk) for hh in range(ML_HPP) for d in range(2)] + latent

    def scores(item):
        hh, _, c = item
        return jnp.dot(k_s[hh, c * L:(c + 1) * L, :], qt_s[hh, c], preferred_element_type=F32)

    def decayed(item, st):
        hh, d, c = item
        _, cmax_row, _, m_old, _ = gates[item]
        mm = jnp.maximum(cmax_row, m_old)
        bcol = bcol_s[hh * 2 + d, c]
        logd = jnp.where(visible[d], jnp.concatenate([bcol] * (L // ML_HD), axis=1) - mm, -jnp.inf)
        return (st * jnp.exp2(logd)).astype(BF16)

    def increment(item):
        hh, d, slot = item
        _, _, f_tot, _, m_new = gates[item]
        kc = k_s[hh, slot * L:(slot + 1) * L, :]
        kw = (kc.astype(F32) * jnp.exp2(bcol_s[hh * 2 + d, slot] + (f_tot - m_new))).astype(BF16)
        upd_s[hh * 2 + d, slot] = jnp.dot(vat_s[hh, slot], kw, preferred_element_type=F32)

    n_lat = len(latent)
    s_val, p_val = {}, {}
    for t in range(n_lat + 2):
        if t < n_lat:
            s_val[t] = scores(latent[t])
        if 0 <= t - 1 < n_lat:
            p_val[t - 1] = decayed(latent[t - 1], s_val.pop(t - 1))
        if 0 <= t - 2 < n_lat:
            hh, d, c = latent[t - 2]
            r1_s[hh * 2 + d, c] = jnp.dot(vat_s[hh, c], p_val.pop(t - 2), preferred_element_type=F32)
        if t < len(every):
            increment(every[t])
    for t in range(n_lat + 2, len(every)):
        increment(every[t])

    for hh in range(ML_HPP):
        for d in range(2):
            sd = hh * 2 + d
            st = upd_s[sd, nchunk]
            for c in (range(nchunk) if d == 0 else range(nchunk - 1, -1, -1)):
                _, _, f_tot, m_old, m_new = gates[hh, d, c]
                stb_s[sd, c] = st.astype(BF16)
                st = jnp.exp2(f_tot + m_old - m_new) * st + upd_s[sd, c]

    def carried(item):
        hh, d, c = item
        return jnp.dot(stb_s[hh * 2 + d, c], qt_s[hh, c], preferred_element_type=F32)

    def emit(item, r2):
        hh, d, c = item
        a_row, cmax_row, _, m_old, _ = gates[item]
        mm = jnp.maximum(cmax_row, m_old)
        tot = r1_s[hh * 2 + d, c] + jnp.exp2(m_old - mm) * r2
        den = tot[ML_HD:ML_HD + 1, :]
        floor = jnp.exp2(-(a_row + mm))
        (hf_s if d == 0 else hb_s)[hh, c] = tot[0:ML_HD, :] * (1.0 / jnp.maximum(jnp.abs(den), floor))

    r2_val = {}
    for t in range(n_lat + 1):
        if t < n_lat:
            r2_val[t] = carried(latent[t])
        if t >= 1:
            emit(latent[t - 1], r2_val.pop(t - 1))


    for hh in range(ML_HPP):
        cols_h = slice(hh * ML_HD, (hh + 1) * ML_HD)
        for c in range(nchunk):
            ht = hf_s[hh, c] + hb_s[hh, c]
            ms = jnp.mean(ht * ht, axis=0, keepdims=True)
            hn = (ht * lax.rsqrt(ms + EPS)).T * hg_ref[:, cols_h]
            rows = slice(c * L, (c + 1) * L)
            y = ((ob_ref[0, rows, cols_h].astype(F32) * hn + sk_ref[:, cols_h] * xcv_s[hh, rows, :])
                 * zb_ref[0, rows, cols_h].astype(F32))
            o_ref[0, rows, cols_h] = y.astype(BF16)


def _mlstm(xm_l, xm_c, vt_all, gt_l, gt_c, zb, ob, conv_w, conv_b, wqt, wk, head_g, skip):
    b, t, _ = xm_l.shape
    n_ctx = xm_c.shape[1]
    nchunk = t // ML_L
    nslot = nchunk + 1
    assert t % (8 * ML_L) == 0 and n_ctx == ML_L and vt_all.shape[1] == nslot
    gl = gt_l.reshape(b, ML_GATES, nchunk, ML_L)
    gc = gt_c.reshape(b, ML_GATES, 1, ML_L)
    pw = ML_HPP * ML_HD
    vt_blk0 = DA_WIDTH // pw
    tokmap = lambda bi, h: (bi, 0, h)
    gmap = lambda bi, h: (bi, 0, 0, 0)
    hvec = lambda bi, h: (0, h)
    wmap = lambda bi, h: (h, 0, 0)
    return pl.pallas_call(
        _mlstm_kernel,
        grid=(b, ML_HEADS // ML_HPP),
        in_specs=[pl.BlockSpec((1, t, pw), tokmap),
                  pl.BlockSpec((1, n_ctx, pw), tokmap),
                  pl.BlockSpec((1, nslot, pw, ML_L), lambda bi, h: (bi, 0, vt_blk0 + h, 0)),
                  pl.BlockSpec((1, ML_GATES, nchunk, ML_L), gmap),
                  pl.BlockSpec((1, ML_GATES, 1, ML_L), gmap),
                  pl.BlockSpec((1, t, pw), tokmap),
                  pl.BlockSpec((1, t, pw), tokmap),
                  pl.BlockSpec((conv_w.shape[0], pw), hvec),
                  pl.BlockSpec((1, pw), hvec),
                  pl.BlockSpec((ML_HPP, ML_HD, ML_HD), wmap),
                  pl.BlockSpec((ML_HPP, ML_HD, ML_HD), wmap),
                  pl.BlockSpec((1, pw), hvec),
                  pl.BlockSpec((1, pw), hvec)],
        out_specs=pl.BlockSpec((1, t, pw), tokmap),
        out_shape=jax.ShapeDtypeStruct((b, t, ML_WIDTH), BF16),
        scratch_shapes=[pltpu.VMEM((ML_HPP, t, ML_HD), F32),
                        pltpu.VMEM((ML_HPP, nslot, ML_HD, ML_L), BF16),
                        pltpu.VMEM((ML_HPP, nslot * ML_L, ML_HD), BF16),
                        pltpu.VMEM((ML_HPP, nslot, VA_ROWS, ML_L), BF16),
                        pltpu.VMEM((ML_HPP, nchunk, ML_HD, ML_L), F32),
                        pltpu.VMEM((ML_HPP, nchunk, ML_HD, ML_L), F32),
                        pltpu.VMEM((ML_HPP * 2, nslot, VA_ROWS, ML_HD), F32),
                        pltpu.VMEM((ML_HPP * 2, nchunk, VA_ROWS, ML_HD), BF16),
                        pltpu.VMEM((ML_HPP * 2, nchunk, VA_ROWS, ML_L), F32),
                        pltpu.VMEM((ML_HPP * 2, nslot, ML_L, ML_HD), F32),
                        pltpu.VMEM((t + 16, ML_HD), F32),
                        pltpu.VMEM((t, ML_HD), BF16)],
        compiler_params=pltpu.CompilerParams(dimension_semantics=("parallel", "arbitrary"),
                                             vmem_limit_bytes=VMEM_LIMIT),
        name="mlstm",
    )(xm_l, xm_c, vt_all, gl, gc, zb, ob, conv_w, conv_b, wqt, wk, head_g, skip)


def _outproj_kernel(x_ref, ya_ref, yb_ref, gg_ref, gt_ref, woa_ref, wob_ref, wo_ref, o_ref):
    d = x_ref.shape[2]
    a = jnp.dot(ya_ref[0], woa_ref[...], preferred_element_type=F32)
    b = jnp.dot(yb_ref[0], wob_ref[...], preferred_element_type=F32)
    y = gg_ref[0, :, 0:d].astype(F32) * a + gg_ref[0, :, d:2 * d].astype(F32) * b
    o = jnp.dot(y.astype(BF16), wo_ref[...], preferred_element_type=F32)
    o_ref[0] = x_ref[0] + gt_ref[0] * o


def _out_proj(x, ya, yb, gg, gt, woa, wob, wo):
    b, t, d = x.shape
    tm = min(OUT_TILE, t)
    tok = lambda width: pl.BlockSpec((1, tm, width), lambda bi, i: (bi, i, 0))
    const2 = lambda bi, i: (0, 0)
    return pl.pallas_call(
        _outproj_kernel,
        grid=(b, t // tm),
        in_specs=[tok(d), tok(DA_WIDTH), tok(ML_WIDTH), tok(2 * d),
                  pl.BlockSpec((1, 1, d), lambda bi, i: (bi, 0, 0)),
                  pl.BlockSpec(woa.shape, const2), pl.BlockSpec(wob.shape, const2),
                  pl.BlockSpec(wo.shape, const2)],
        out_specs=tok(d),
        out_shape=jax.ShapeDtypeStruct((b, t, d), x.dtype),
        compiler_params=pltpu.CompilerParams(dimension_semantics=("parallel", "parallel"),
                                             vmem_limit_bytes=VMEM_LIMIT),
        name="out_proj",
    )(x, ya, yb, gg, gt, woa, wob, wo)


def _rope_tables(n_tokens):
    rows = n_tokens // GRID_W
    row_id = jnp.repeat(jnp.arange(rows, dtype=F32), GRID_W)
    col_id = jnp.tile(jnp.arange(GRID_W, dtype=F32), rows)
    n_freq = DA_HD // 4
    inv_freq = ROPE_THETA ** (-jnp.arange(n_freq, dtype=F32) / n_freq)
    ang_r = row_id[:, None] * inv_freq
    ang_c = col_id[:, None] * inv_freq
    cos = jnp.concatenate([jnp.cos(ang_r)] * 2 + [jnp.cos(ang_c)] * 2, axis=-1)
    sin = jnp.concatenate([-jnp.sin(ang_r), jnp.sin(ang_r), -jnp.sin(ang_c), jnp.sin(ang_c)], axis=-1)
    return jnp.tile(cos, (1, LANES // DA_HD)), jnp.tile(sin, (1, LANES // DA_HD))


def kernel(x, c, ctx, c_ctx, norm_w, w_mod, b_mod, w_in, b_if, da_q_norm, da_k_norm, da_lambda_q1, da_lambda_k1, da_lambda_q2, da_lambda_k2, da_head_norm, w_out_a, ml_conv_w, ml_conv_b, ml_wq, ml_wk, ml_head_norm, ml_skip, w_out_b, w_o):
    assert w_mod.shape[0] == 1, "single-layer block"
    b, t, d = x.shape

    rows = -(-(b + 1) // 8) * 8
    cc = jnp.zeros((rows, d), F32).at[:b].set(c).at[b].set(c_ctx)
    mod, lam, bound = _modulation(cc, w_mod[0], b_mod, da_lambda_q1, da_lambda_k1, da_lambda_q2, da_lambda_k2,
                                  da_q_norm, da_k_norm)
    sh_l, sc_l, gt_l = (mod[:b, i * d:(i + 1) * d].reshape(b, 1, d) for i in range(3))
    sh_c, sc_c = (mod[b:b + 1, i * d:(i + 1) * d].reshape(1, 1, d) for i in range(2))

    w = w_in[0]
    w_main = w.astype(BF16)
    w_gates = jnp.pad(w[:, N_MAIN:N_MAIN + ML_GATES], ((0, 0), (0, LANES - ML_GATES)))
    w_tail = jnp.concatenate([w[:, N_MAIN + ML_GATES:], w_gates], axis=1).astype(BF16)
    w_vt = jnp.concatenate([w_main[:, COL_VA:COL_ZA], w_main[:, COL_VM:COL_ZB]], axis=1).T
    bif = b_if.reshape(ML_GATES, 1)

    grp = np.arange(DA_WIDTH) // DA_HD
    bd = jnp.asarray((grp[:, None] == grp[None, :]).astype(np.float32) / DA_HD, dtype=BF16)
    gq = jnp.tile(da_q_norm, (1, DA_WIDTH // DA_HD))
    gk = jnp.tile(da_k_norm, (1, DA_WIDTH // DA_HD))
    cos, sin = _rope_tables(t)

    n_keys = t + ctx.shape[1]
    qa, k_all, vt_all, za, xm_l, zb, ob, gg, gtl = _in_proj(
        x, sc_l, sh_l, norm_w, w_main, w_tail, w_vt, bif, (bd, gq, gk, cos, sin), latent=True, n_keys=n_keys, key_off=0)
    k_all, vt_all, xm_c, gtc = _in_proj(
        ctx, sc_c, sh_c, norm_w, w_main, w_tail, w_vt, bif, (bd, gk, k_all, vt_all), latent=False, n_keys=n_keys, key_off=t)

    ya = _diff_attn(bound[0, :1], qa, k_all, vt_all, za, lam, da_head_norm)
    yb = _mlstm(xm_l, xm_c, vt_all, gtl, gtc, zb, ob, ml_conv_w[0], ml_conv_b,
                jnp.swapaxes(ml_wq[0], 1, 2).astype(BF16), ml_wk[0].astype(BF16), ml_head_norm, ml_skip)
    return _out_proj(x, ya, yb, gg, gt_l, w_out_a[0].astype(BF16), w_out_b[0].astype(BF16),
                     w_o[0].astype(BF16))
```

```python
import functools

import jax
import jax.numpy as jnp
import numpy as np
from jax import lax
from jax.experimental import pallas as pl
from jax.experimental.pallas import tpu as pltpu

F32 = jnp.float32
BF16 = jnp.bfloat16

EPS = 1e-6
ROPE_THETA = 10000.0
GRID_W = 64
LAM_INIT = 0.8 - 0.6 * 1.0

DA_HEADS = 4
DA_HD = 64
DA_HEAD_W = 2 * DA_HD
DA_WIDTH = DA_HEADS * DA_HEAD_W

ML_HEADS = 4
ML_HD = 128
ML_WIDTH = ML_HEADS * ML_HD
ML_GATES = 4 * ML_HEADS

COL_QA, COL_KA, COL_VA, COL_ZA = (i * DA_WIDTH for i in range(4))
COL_XM, COL_VM, COL_ZB, COL_OB = (4 * DA_WIDTH + i * ML_WIDTH for i in range(4))
N_MAIN = 4 * DA_WIDTH + 4 * ML_WIDTH
ML_L = 256
VA_ROWS = ML_HD + 16
LOG2E = 1.4426950408889634
ML_HPP = 2

LANES = 128
VMEM_LIMIT = 56 * 1024 * 1024

IN_TILE = 512
OUT_TILE = 1024
ATT_BLK = 256
ATT_KC = 256
ATT_UNROLL = 3
ATT_EXP2_HEADROOM = 30.0
SCORE_BOUND_MARGIN = 1.02
Q_SCALE = DA_HD ** -0.5 * LOG2E


def _sigmoid(x):
    return 0.5 * jnp.tanh(0.5 * x) + 0.5


def _silu(x):
    h = 0.5 * x
    return h * jnp.tanh(h) + h


def _split3(x):
    hi = x.astype(BF16)
    r1 = x - hi.astype(F32)
    mid = r1.astype(BF16)
    lo = (r1 - mid.astype(F32)).astype(BF16)
    return hi, mid, lo


def _mod_kernel(cc_ref, w_ref, b_ref, lq1_ref, lk1_ref, lq2_ref, lk2_ref, gq_ref, gk_ref,
                mod_ref, lam_ref, bound_ref):
    a = _silu(cc_ref[...]).astype(BF16)
    mod_ref[...] = jnp.dot(a, w_ref[...].astype(BF16), preferred_element_type=F32) + b_ref[...]
    s1 = jnp.sum(lq1_ref[...] * lk1_ref[...], axis=-1, keepdims=True)
    s2 = jnp.sum(lq2_ref[...] * lk2_ref[...], axis=-1, keepdims=True)
    lam = jnp.exp(s1) - jnp.exp(s2) + LAM_INIT
    lam_ref[...] = jnp.broadcast_to(lam, lam_ref.shape)
    bq = jnp.max(jnp.abs(gq_ref[...]), axis=-1, keepdims=True)
    bk = jnp.max(jnp.abs(gk_ref[...]), axis=-1, keepdims=True)
    bound_ref[...] = jnp.broadcast_to(SCORE_BOUND_MARGIN * Q_SCALE * DA_HD * bq * bk, bound_ref.shape)


def _modulation(cc, w_mod, b_mod, lq1, lk1, lq2, lk2, gq, gk):
    rows, d = cc.shape
    n3 = w_mod.shape[1]
    vec = pl.BlockSpec((1, DA_HD), lambda j: (0, 0))
    return pl.pallas_call(
        _mod_kernel,
        grid=(n3 // d,),
        in_specs=[pl.BlockSpec((rows, d), lambda j: (0, 0)),
                  pl.BlockSpec((d, d), lambda j: (0, j)),
                  pl.BlockSpec((1, d), lambda j: (0, j)),
                  vec, vec, vec, vec, vec, vec],
        out_specs=[pl.BlockSpec((rows, d), lambda j: (0, j)),
                   pl.BlockSpec((1, LANES), lambda j: (0, 0)),
                   pl.BlockSpec((1, LANES), lambda j: (0, 0))],
        out_shape=[jax.ShapeDtypeStruct((rows, n3), F32),
                   jax.ShapeDtypeStruct((1, LANES), F32),
                   jax.ShapeDtypeStruct((1, LANES), F32)],
        compiler_params=pltpu.CompilerParams(dimension_semantics=("arbitrary",),
                                             vmem_limit_bytes=VMEM_LIMIT),
        name="modulation",
    )(cc, w_mod, b_mod, lq1, lk1, lq2, lk2, gq, gk)


def _group_rms(acc, bd_ref, g_ref):
    ms = jnp.dot((acc * acc).astype(BF16), bd_ref[...], preferred_element_type=F32)
    return acc * lax.rsqrt(ms + EPS) * g_ref[...]


def _rope(x, cos, sin_signed, first_half):
    outs = []
    for c in range(x.shape[1] // LANES):
        xs = x[:, c * LANES:(c + 1) * LANES]
        partner = jnp.where(first_half, pltpu.roll(xs, LANES - 16, 1), pltpu.roll(xs, 16, 1))
        outs.append(xs * cos + partner * sin_signed)
    return jnp.concatenate(outs, axis=1)


def _inproj_kernel(x_ref, sc_ref, sh_ref, nw_ref, w_ref, wt_ref, wvt_ref, bif_ref, *rest, latent):
    if latent:
        (bd_ref, gq_ref, gk_ref, cos_ref, sin_ref,
         qa_ref, ka_ref, vt_ref, za_ref, xm_ref, zb_ref, ob_ref, gg_ref, gt_ref) = rest
    else:
        bd_ref, gk_ref, _, _, ka_ref, vt_ref, xm_ref, gt_ref = rest

    slab = ATT_KC
    nslab = x_ref.shape[1] // slab

    def normed(j):
        xf = x_ref[0, j * slab:(j + 1) * slab, :]
        ms = jnp.mean(xf * xf, axis=-1, keepdims=True)
        y = xf * lax.rsqrt(ms + EPS) * nw_ref[...]
        return (y * (1.0 + sc_ref[0]) + sh_ref[0]).astype(BF16)

    def values(j, h):
        vt_ref[0, j] = lax.dot_general(wvt_ref[...], h, (((1,), (1,)), ((), ())),
                                       preferred_element_type=F32).astype(BF16)

    def projections(j, h):
        rows = slice(j * slab, (j + 1) * slab)

        def proj(ref, c0, width):
            return jnp.dot(h, ref[:, c0:c0 + width], preferred_element_type=F32)

        if latent:
            lane = lax.broadcasted_iota(jnp.int32, (slab, LANES), 1)
            first_half = (lane % 32) < 16
            cos = cos_ref[rows, :]
            sin = sin_ref[rows, :]
            q = _rope(_group_rms(proj(w_ref, COL_QA, DA_WIDTH), bd_ref, gq_ref), cos, sin, first_half)
            qa_ref[0, rows, :] = (q * Q_SCALE).astype(BF16)
            k = _rope(_group_rms(proj(w_ref, COL_KA, DA_WIDTH), bd_ref, gk_ref), cos, sin, first_half)
            ka_ref[0, rows, :] = k.astype(BF16)
            za_ref[0, rows, :] = _silu(proj(w_ref, COL_ZA, DA_WIDTH)).astype(BF16)
            zb_ref[0, rows, :] = _silu(proj(w_ref, COL_ZB, ML_WIDTH)).astype(BF16)
            ob_ref[0, rows, :] = _sigmoid(proj(w_ref, COL_OB, ML_WIDTH)).astype(BF16)
            for c in range(gg_ref.shape[2] // 512):
                gg_ref[0, rows, c * 512:(c + 1) * 512] = _sigmoid(proj(wt_ref, c * 512, 512)).astype(BF16)
        else:
            ka_ref[0, rows, :] = _group_rms(proj(w_ref, COL_KA, DA_WIDTH), bd_ref, gk_ref).astype(BF16)

        g = proj(wt_ref, wt_ref.shape[1] - LANES, LANES).T[:ML_GATES] + bif_ref[...]
        row = lax.broadcasted_iota(jnp.int32, g.shape, 0)
        logsig = jnp.minimum(g, 0.0) - jnp.log(1.0 + jnp.exp(-jnp.abs(g)))
        gt_ref[0, :, rows] = jnp.where((row // ML_HEADS) % 2 == 1, logsig, g)

        xm_ref[0, rows, :] = proj(w_ref, COL_XM, ML_WIDTH).astype(BF16)

    h = normed(0)
    for j in range(nslab):
        values(j, h)
        h_next = normed(j + 1) if j + 1 < nslab else None
        projections(j, h)
        h = h_next


def _in_proj(x, sc, sh, norm_w, w, w_tail, wvt, bif, extra, *, latent, n_keys, key_off):
    b, t, d = x.shape
    tm = min(IN_TILE, t)
    assert key_off % tm == 0
    kblk = key_off // tm
    grid = (b, t // tm)
    const2 = lambda bi, i: (0, 0)
    mod_map = (lambda bi, i: (bi, 0, 0)) if latent else (lambda bi, i: (0, 0, 0))
    tok = lambda width, dtype: (pl.BlockSpec((1, tm, width), lambda bi, i: (bi, i, 0)),
                                jax.ShapeDtypeStruct((b, t, width), dtype))
    keys = (pl.BlockSpec((1, tm, DA_WIDTH), lambda bi, i: (bi, kblk + i, 0)),
            jax.ShapeDtypeStruct((b, n_keys, DA_WIDTH), BF16))
    cpt = tm // ATT_KC
    vals = (pl.BlockSpec((1, cpt, wvt.shape[0], ATT_KC), lambda bi, i: (bi, kblk + i, 0, 0)),
            jax.ShapeDtypeStruct((b, n_keys // ATT_KC, wvt.shape[0], ATT_KC), BF16))
    w_tail_spec = (pl.BlockSpec(w_tail.shape, const2) if latent else
                   pl.BlockSpec((d, LANES), lambda bi, i: (0, w_tail.shape[1] // LANES - 1)))
    in_specs = [pl.BlockSpec((1, tm, d), lambda bi, i: (bi, i, 0)),
                pl.BlockSpec((1, 1, d), mod_map),
                pl.BlockSpec((1, 1, d), mod_map),
                pl.BlockSpec((1, d), const2),
                pl.BlockSpec((d, N_MAIN), const2),
                w_tail_spec,
                pl.BlockSpec(wvt.shape, const2),
                pl.BlockSpec(bif.shape, const2)]
    if latent:
        bd, gq, gk, cos, sin = extra
        in_specs += [pl.BlockSpec(bd.shape, const2), pl.BlockSpec(gq.shape, const2),
                     pl.BlockSpec(gk.shape, const2),
                     pl.BlockSpec((tm, LANES), lambda bi, i: (i, 0)),
                     pl.BlockSpec((tm, LANES), lambda bi, i: (i, 0))]
        outs = ([tok(DA_WIDTH, BF16), keys, vals, tok(DA_WIDTH, BF16)] + [tok(ML_WIDTH, BF16)] * 3
                + [tok(2 * d, BF16)])
        aliases = {}
    else:
        bd, gk, _, _ = extra
        in_specs += [pl.BlockSpec(bd.shape, const2), pl.BlockSpec(gk.shape, const2),
                     pl.BlockSpec(memory_space=pl.ANY), pl.BlockSpec(memory_space=pl.ANY)]
        outs = [keys, vals, tok(ML_WIDTH, BF16)]
        aliases = {len(in_specs) - 2: 0, len(in_specs) - 1: 1}
    outs.append((pl.BlockSpec((1, ML_GATES, tm), lambda bi, i: (bi, 0, i)),
                 jax.ShapeDtypeStruct((b, ML_GATES, t), F32)))
    return pl.pallas_call(
        functools.partial(_inproj_kernel, latent=latent),
        grid=grid,
        in_specs=in_specs,
        out_specs=[o[0] for o in outs],
        out_shape=[o[1] for o in outs],
        input_output_aliases=aliases,
        compiler_params=pltpu.CompilerParams(dimension_semantics=("parallel", "parallel"),
                                             vmem_limit_bytes=VMEM_LIMIT),
        name="in_proj_latent" if latent else "in_proj_context",
    )(x, sc, sh, norm_w, w, w_tail, wvt, bif, *extra)


def _attn_kernel(bound_ref, q_ref, k_ref, vt_ref, z_ref, lam_ref, hg_ref, o_ref, s_scr, e_scr, acc_scr, qm_scr):
    nblk = q_ref.shape[1] // ATT_BLK
    nkc = k_ref.shape[1] // ATT_KC
    sub = ATT_KC // 8
    lam = lam_ref[:, 0:1]
    nt = (((1,), (1,)), ((), ()))
    neg = jnp.full((8, ATT_BLK), -jnp.inf, F32)
    zero8 = jnp.zeros((8, ATT_BLK), F32)
    bound = bound_ref[0]

    def mask_queries(j):
        q = q_ref[0, j * ATT_BLK:(j + 1) * ATT_BLK, :]
        lane = lax.broadcasted_iota(jnp.int32, q.shape, 1)
        zero = jnp.zeros_like(q)
        qm_scr[0] = jnp.where(lane < DA_HD, q, zero)
        qm_scr[1] = jnp.where(lane >= DA_HD, q, zero)

    def scores(c, mp):
        kc = k_ref[0, pl.ds(pl.multiple_of(c * ATT_KC, ATT_KC), ATT_KC), :]
        return lax.dot_general(kc, qm_scr[mp], nt, preferred_element_type=F32).reshape(sub, 8, ATT_BLK)

    def exps(c, par, mp, st, stab8, lacc):
        e = jnp.exp2(st - stab8[None])
        e_scr[par, mp, c] = e.reshape(ATT_KC, ATT_BLK)
        return lacc + jnp.sum(e, axis=0)

    def values(c, par, r8):
        e1 = e_scr[par, 0, c].reshape(sub, 8, ATT_BLK)
        e2 = e_scr[par, 1, c].reshape(sub, 8, ATT_BLK)
        pt = (e1 - e2 * r8[None]).reshape(ATT_KC, ATT_BLK).astype(BF16)
        acc_scr[...] += jnp.dot(vt_ref[0, c], pt, preferred_element_type=F32)

    def finish(j, inv_l1):
        o = (acc_scr[...] * inv_l1).T
        ms = jnp.mean(o * o, axis=-1, keepdims=True)
        on = o * lax.rsqrt(ms + EPS) * hg_ref[...]
        rows = slice(j * ATT_BLK, (j + 1) * ATT_BLK)
        o_ref[0, rows, :] = (on * (1.0 - LAM_INIT) * z_ref[0, rows, :].astype(F32)).astype(BF16)

    def pipeline(exact_max):
        d_exp = 1 if exact_max else 0
        d_val = d_exp + 1
        bound8 = jnp.full((8, ATT_BLK), bound, F32)
        m8, r8, inv_l1 = {}, {}, {}
        for j in range(nblk + d_val):
            do_s, do_e, do_v = j < nblk, d_exp <= j < nblk + d_exp, d_val <= j
            if do_s:
                mask_queries(j)
            if do_v:
                acc_scr[...] = jnp.zeros_like(acc_scr)

            def body(c, carry, j=j, do_s=do_s, do_e=do_e, do_v=do_v):
                macc, lacc = carry
                if exact_max:
                    if do_s:
                        macc = list(macc)
                        for mp in range(2):
                            st = scores(c, mp)
                            s_scr[j % 2, mp, c] = st.reshape(ATT_KC, ATT_BLK)
                            macc[mp] = jnp.maximum(macc[mp], jnp.max(st, axis=0))
                        macc = tuple(macc)
                    if do_e:
                        jb = j - d_exp
                        lacc = tuple(
                            exps(c, jb % 2, mp, s_scr[jb % 2, mp, c].reshape(sub, 8, ATT_BLK), m8[jb][mp], lacc[mp])
                            for mp in range(2))
                elif do_s:
                    lacc = tuple(exps(c, j % 2, mp, scores(c, mp), bound8, lacc[mp]) for mp in range(2))
                if do_v:
                    values(c, (j - d_val) % 2, r8[j - d_val])
                return macc, lacc

            macc, lacc = lax.fori_loop(0, nkc, body, ((neg, neg), (zero8, zero8)),
                                       unroll=ATT_UNROLL if exact_max else nkc)
            if do_v:
                finish(j - d_val, inv_l1.pop(j - d_val))
                r8.pop(j - d_val)
            if do_e:
                l1 = jnp.sum(lacc[0], axis=0, keepdims=True)
                l2 = jnp.sum(lacc[1], axis=0, keepdims=True)
                r8[j - d_exp] = jnp.broadcast_to(lam * l1 / l2, (8, ATT_BLK))
                inv_l1[j - d_exp] = 1.0 / l1
                m8.pop(j - d_exp, None)
            if exact_max and do_s:
                m8[j] = tuple(jnp.broadcast_to(jnp.max(a, axis=0, keepdims=True), (8, ATT_BLK)) for a in macc)

    use_bound = bound <= 0.5 * (126.0 - ATT_EXP2_HEADROOM)

    @pl.when(use_bound)
    def _():
        pipeline(False)

    @pl.when(jnp.logical_not(use_bound))
    def _():
        pipeline(True)


def _diff_attn(bound, qa, k_all, vt_all, za, lam, head_g):
    b, t, _ = qa.shape
    n_keys = k_all.shape[1]
    nkc = n_keys // ATT_KC
    assert t % ATT_BLK == 0 and n_keys % ATT_KC == 0 and nkc % ATT_UNROLL == 0
    qmap = lambda bi, h: (bi, 0, h)
    const2 = lambda bi, h: (0, 0)
    return pl.pallas_call(
        _attn_kernel,
        grid=(b, DA_HEADS),
        in_specs=[pl.BlockSpec(memory_space=pltpu.SMEM),
                  pl.BlockSpec((1, t, DA_HEAD_W), qmap),
                  pl.BlockSpec((1, n_keys, DA_HEAD_W), qmap),
                  pl.BlockSpec((1, nkc, DA_HEAD_W, ATT_KC), lambda bi, h: (bi, 0, h, 0)),
                  pl.BlockSpec((1, t, DA_HEAD_W), qmap),
                  pl.BlockSpec((1, LANES), const2),
                  pl.BlockSpec((1, DA_HEAD_W), const2)],
        out_specs=pl.BlockSpec((1, t, DA_HEAD_W), qmap),
        out_shape=jax.ShapeDtypeStruct((b, t, DA_WIDTH), BF16),
        scratch_shapes=[pltpu.VMEM((2, 2, nkc, ATT_KC, ATT_BLK), F32),
                        pltpu.VMEM((2, 2, nkc, ATT_KC, ATT_BLK), F32),
                        pltpu.VMEM((DA_HEAD_W, ATT_BLK), F32),
                        pltpu.VMEM((2, ATT_BLK, DA_HEAD_W), BF16)],
        compiler_params=pltpu.CompilerParams(dimension_semantics=("parallel", "arbitrary"),
                                             vmem_limit_bytes=VMEM_LIMIT),
        name="diff_attn",
    )(bound, qa, k_all, vt_all, za, lam, head_g)


def _conv_silu(x, w, b, pad_s):
    t = x.shape[0]
    edge = jnp.zeros((8, x.shape[1]), F32)
    pad_s[0:8, :] = edge
    pad_s[8:8 + t, :] = x.astype(F32)
    pad_s[8 + t:16 + t, :] = edge
    y = b + w[0:1, :] * pad_s[7:7 + t, :]
    y = y + w[1:2, :] * pad_s[8:8 + t, :]
    y = y + w[2:3, :] * pad_s[9:9 + t, :]
    return _silu(y)


def _rows_to_cols(rows, eye):
    nt = (((1,), (1,)), ((), ()))
    out = None
    for part in _split3(rows):
        term = lax.dot_general(eye, part, nt, preferred_element_type=F32)
        out = term if out is None else out + term
    return out


def _cumsum_rows(rows, tri):
    out = None
    for part in _split3(rows):
        term = jnp.dot(part, tri, preferred_element_type=F32)
        out = term if out is None else out + term
    return out


def _cummax_rows(rows, reverse):
    n = rows.shape[1]
    lane = lax.broadcasted_iota(jnp.int32, rows.shape, 1)
    sh = 1
    while sh < n:
        if reverse:
            shifted = jnp.where(lane < n - sh, pltpu.roll(rows, n - sh, 1), -jnp.inf)
        else:
            shifted = jnp.where(lane >= sh, pltpu.roll(rows, sh, 1), -jnp.inf)
        rows = jnp.maximum(rows, shifted)
        sh *= 2
    return rows


def _mlstm_kernel(xl_ref, xc_ref, vt_ref, gl_ref, gc_ref, zb_ref, ob_ref,
                  cw_ref, cb_ref, wqt_ref, wk_ref, hg_ref, sk_ref, o_ref,
                  xcv_s, qt_s, k_s, vat_s, hf_s, hb_s, upd_s, stb_s, r1_s, bcol_s, pad_s, xb_s):
    hp = pl.program_id(1)
    n_lat = xl_ref.shape[1]
    nchunk = n_lat // ML_L
    L = ML_L
    nt = (((1,), (1,)), ((), ()))

    r_i = lax.broadcasted_iota(jnp.int32, (L, L), 0)
    c_i = lax.broadcasted_iota(jnp.int32, (L, L), 1)
    eye = jnp.where(r_i == c_i, 1.0, 0.0).astype(BF16)
    tris = (jnp.where(r_i <= c_i, 1.0, 0.0).astype(BF16), jnp.where(r_i >= c_i, 1.0, 0.0).astype(BF16))
    visible = (r_i <= c_i, r_i >= c_i)
    row = lax.broadcasted_iota(jnp.int32, (VA_ROWS - ML_HD, L), 0)
    ones_row = jnp.where(row == 0, 1.0, 0.0).astype(BF16)
    lane8 = lax.broadcasted_iota(jnp.int32, (8, L), 1)
    zero11 = jnp.zeros((1, 1), F32)

    gates = {}
    per_slot = {}

    def prepare(hh, x_ref, slot0):
        cols_h = slice(hh * ML_HD, (hh + 1) * ML_HD)
        n = x_ref.shape[1]
        xc = _conv_silu(x_ref[0, :, cols_h], cw_ref[:, cols_h], cb_ref[:, cols_h], pad_s)
        xb_s[0:n, :] = xc.astype(BF16)
        xb = xb_s[0:n, :]
        qt = lax.dot_general(wqt_ref[hh], xb, nt, preferred_element_type=F32).astype(BF16)
        for j in range(n // L):
            qt_s[hh, slot0 + j] = qt[:, j * L:(j + 1) * L]
        k = jnp.dot(xb, wk_ref[hh], preferred_element_type=F32) * (ML_HD ** -0.5)
        k_s[hh, slot0 * L:slot0 * L + n, :] = k.astype(BF16)
        return xc

    def gate_algebra(hh, d, g_ref, r0):
        head = hp * ML_HPP + hh
        ig = g_ref[0, (2 * d) * ML_HEADS + head] * LOG2E
        lf = g_ref[0, (2 * d + 1) * ML_HEADS + head] * LOG2E
        n = ig.shape[0]
        if n == 1:
            ig = jnp.broadcast_to(ig, (8, L))
            lf = jnp.broadcast_to(lf, (8, L))
        a = _cumsum_rows(lf, tris[d])
        bb = ig - a
        pick = (lane8 == L - 1) if d == 0 else (lane8 == 0)
        f_tot = jnp.sum(jnp.where(pick, a, 0.0), axis=-1, keepdims=True)
        b_max = jnp.max(bb, axis=-1, keepdims=True)
        cmax = _cummax_rows(bb, d == 1)
        cols = _rows_to_cols(bb, eye)
        for j in range(n):
            bcol_s[hh * 2 + d, r0 + j] = jnp.broadcast_to(cols[:, j:j + 1], (L, ML_HD))
            per_slot[hh, d, r0 + j] = (a[j:j + 1], cmax[j:j + 1], f_tot[j:j + 1], b_max[j:j + 1])

    for hh in range(ML_HPP):
        gate_algebra(hh, 0, gl_ref, 0)
        prepare(hh, xc_ref, nchunk)
        gate_algebra(hh, 0, gc_ref, nchunk)
        gate_algebra(hh, 1, gl_ref, 0)
        xcv_s[hh] = prepare(hh, xl_ref, 0)
        gate_algebra(hh, 1, gc_ref, nchunk)
        cols_h = slice(hh * ML_HD, (hh + 1) * ML_HD)
        for j in range(nchunk + 1):
            vat_s[hh, j, 0:ML_HD, :] = vt_ref[0, j, cols_h, :]
            vat_s[hh, j, ML_HD:VA_ROWS, :] = ones_row
        for d in range(2):
            m = zero11
            order = [nchunk] + (list(range(nchunk)) if d == 0 else list(range(nchunk - 1, -1, -1)))
            for slot in order:
                a_row, cmax_row, f_tot, b_max = per_slot[hh, d, slot]
                m_new = jnp.maximum(f_tot + m, f_tot + b_max)
                gates[hh, d, slot] = (a_row, cmax_row, f_tot, m, m_new)
                m = m_new

    latent = [(hh, d, c) for hh in range(ML_HPP) for d in range(2) for c in range(nchunk)]
    every = [(hh, d, nchunk) for hh in range(ML_HPP) for d in range(2)] + latent

    def scores(item):
        hh, _, c = item
        return jnp.dot(k_s[hh, c * L:(c + 1) * L, :], qt_s[hh, c], preferred_element_type=F32)

    def decayed(item, st):
        hh, d, c = item
        _, cmax_row, _, m_old, _ = gates[item]
        mm = jnp.maximum(cmax_row, m_old)
        bcol = bcol_s[hh * 2 + d, c]
        logd = jnp.where(visible[d], jnp.concatenate([bcol] * (L // ML_HD), axis=1) - mm, -jnp.inf)
        return (st * jnp.exp2(logd)).astype(BF16)

    def increment(item):
        hh, d, slot = item
        _, _, f_tot, _, m_new = gates[item]
        kc = k_s[hh, slot * L:(slot + 1) * L, :]
        kw = kc * jnp.exp2(bcol_s[hh * 2 + d, slot] + (f_tot - m_new)).astype(BF16)
        upd_s[hh * 2 + d, slot] = jnp.dot(vat_s[hh, slot], kw, preferred_element_type=F32)

    n_lat = len(latent)
    s_val, p_val = {}, {}
    for t in range(n_lat + 2):
        if t < n_lat:
            s_val[t] = scores(latent[t])
        if 0 <= t - 1 < n_lat:
            p_val[t - 1] = decayed(latent[t - 1], s_val.pop(t - 1))
        if 0 <= t - 2 < n_lat:
            hh, d, c = latent[t - 2]
            r1_s[hh * 2 + d, c] = jnp.dot(vat_s[hh, c], p_val.pop(t - 2), preferred_element_type=F32)
        if t < len(every):
            increment(every[t])
    for t in range(n_lat + 2, len(every)):
        increment(every[t])

    for hh in range(ML_HPP):
        for d in range(2):
            sd = hh * 2 + d
            st = upd_s[sd, nchunk]
            for c in (range(nchunk) if d == 0 else range(nchunk - 1, -1, -1)):
                _, _, f_tot, m_old, m_new = gates[hh, d, c]
                stb_s[sd, c] = st.astype(BF16)
                st = jnp.exp2(f_tot + m_old - m_new) * st + upd_s[sd, c]

    def carried(item):
        hh, d, c = item
        return jnp.dot(stb_s[hh * 2 + d, c], qt_s[hh, c], preferred_element_type=F32)

    def emit(item, r2):
        hh, d, c = item
        a_row, cmax_row, _, m_old, _ = gates[item]
        mm = jnp.maximum(cmax_row, m_old)
        tot = r1_s[hh * 2 + d, c] + jnp.exp2(m_old - mm) * r2
        den = tot[ML_HD:ML_HD + 1, :]
        floor = jnp.exp2(-(a_row + mm))
        (hf_s if d == 0 else hb_s)[hh, c] = tot[0:ML_HD, :] * (1.0 / jnp.maximum(jnp.abs(den), floor))

    r2_val = {}
    for t in range(n_lat + 1):
        if t < n_lat:
            r2_val[t] = carried(latent[t])
        if t >= 1:
            emit(latent[t - 1], r2_val.pop(t - 1))


    for hh in range(ML_HPP):
        cols_h = slice(hh * ML_HD, (hh + 1) * ML_HD)
        for c in range(nchunk):
            ht = hf_s[hh, c] + hb_s[hh, c]
            ms = jnp.mean(ht * ht, axis=0, keepdims=True)
            hn = (ht * lax.rsqrt(ms + EPS)).T * hg_ref[:, cols_h]
            rows = slice(c * L, (c + 1) * L)
            y = ((ob_ref[0, rows, cols_h].astype(F32) * hn + sk_ref[:, cols_h] * xcv_s[hh, rows, :])
                 * zb_ref[0, rows, cols_h].astype(F32))
            o_ref[0, rows, cols_h] = y.astype(BF16)


def _mlstm(xm_l, xm_c, vt_all, gt_l, gt_c, zb, ob, conv_w, conv_b, wqt, wk, head_g, skip):
    b, t, _ = xm_l.shape
    n_ctx = xm_c.shape[1]
    nchunk = t // ML_L
    nslot = nchunk + 1
    assert t % (8 * ML_L) == 0 and n_ctx == ML_L and vt_all.shape[1] == nslot
    gl = gt_l.reshape(b, ML_GATES, nchunk, ML_L)
    gc = gt_c.reshape(b, ML_GATES, 1, ML_L)
    pw = ML_HPP * ML_HD
    vt_blk0 = DA_WIDTH // pw
    tokmap = lambda bi, h: (bi, 0, h)
    gmap = lambda bi, h: (bi, 0, 0, 0)
    hvec = lambda bi, h: (0, h)
    wmap = lambda bi, h: (h, 0, 0)
    return pl.pallas_call(
        _mlstm_kernel,
        grid=(b, ML_HEADS // ML_HPP),
        in_specs=[pl.BlockSpec((1, t, pw), tokmap),
                  pl.BlockSpec((1, n_ctx, pw), tokmap),
                  pl.BlockSpec((1, nslot, pw, ML_L), lambda bi, h: (bi, 0, vt_blk0 + h, 0)),
                  pl.BlockSpec((1, ML_GATES, nchunk, ML_L), gmap),
                  pl.BlockSpec((1, ML_GATES, 1, ML_L), gmap),
                  pl.BlockSpec((1, t, pw), tokmap),
                  pl.BlockSpec((1, t, pw), tokmap),
                  pl.BlockSpec((conv_w.shape[0], pw), hvec),
                  pl.BlockSpec((1, pw), hvec),
                  pl.BlockSpec((ML_HPP, ML_HD, ML_HD), wmap),
                  pl.BlockSpec((ML_HPP, ML_HD, ML_HD), wmap),
                  pl.BlockSpec((1, pw), hvec),
                  pl.BlockSpec((1, pw), hvec)],
        out_specs=pl.BlockSpec((1, t, pw), tokmap),
        out_shape=jax.ShapeDtypeStruct((b, t, ML_WIDTH), BF16),
        scratch_shapes=[pltpu.VMEM((ML_HPP, t, ML_HD), F32),
                        pltpu.VMEM((ML_HPP, nslot, ML_HD, ML_L), BF16),
                        pltpu.VMEM((ML_HPP, nslot * ML_L, ML_HD), BF16),
                        pltpu.VMEM((ML_HPP, nslot, VA_ROWS, ML_L), BF16),
                        pltpu.VMEM((ML_HPP, nchunk, ML_HD, ML_L), F32),
                        pltpu.VMEM((ML_HPP, nchunk, ML_HD, ML_L), F32),
                        pltpu.VMEM((ML_HPP * 2, nslot, VA_ROWS, ML_HD), F32),
                        pltpu.VMEM((ML_HPP * 2, nchunk, VA_ROWS, ML_HD), BF16),
                        pltpu.VMEM((ML_HPP * 2, nchunk, VA_ROWS, ML_L), F32),
                        pltpu.VMEM((ML_HPP * 2, nslot, ML_L, ML_HD), F32),
                        pltpu.VMEM((t + 16, ML_HD), F32),
                        pltpu.VMEM((t, ML_HD), BF16)],
        compiler_params=pltpu.CompilerParams(dimension_semantics=("parallel", "arbitrary"),
                                             vmem_limit_bytes=VMEM_LIMIT),
        name="mlstm",
    )(xm_l, xm_c, vt_all, gl, gc, zb, ob, conv_w, conv_b, wqt, wk, head_g, skip)


def _outproj_kernel(x_ref, ya_ref, yb_ref, gg_ref, gt_ref, woa_ref, wob_ref, wo_ref, o_ref):
    d = x_ref.shape[2]
    a = jnp.dot(ya_ref[0], woa_ref[...], preferred_element_type=F32)
    b = jnp.dot(yb_ref[0], wob_ref[...], preferred_element_type=F32)
    y = gg_ref[0, :, 0:d].astype(F32) * a + gg_ref[0, :, d:2 * d].astype(F32) * b
    o = jnp.dot(y.astype(BF16), wo_ref[...], preferred_element_type=F32)
    o_ref[0] = x_ref[0] + gt_ref[0] * o


def _out_proj(x, ya, yb, gg, gt, woa, wob, wo):
    b, t, d = x.shape
    tm = min(OUT_TILE, t)
    tok = lambda width: pl.BlockSpec((1, tm, width), lambda bi, i: (bi, i, 0))
    const2 = lambda bi, i: (0, 0)
    return pl.pallas_call(
        _outproj_kernel,
        grid=(b, t // tm),
        in_specs=[tok(d), tok(DA_WIDTH), tok(ML_WIDTH), tok(2 * d),
                  pl.BlockSpec((1, 1, d), lambda bi, i: (bi, 0, 0)),
                  pl.BlockSpec(woa.shape, const2), pl.BlockSpec(wob.shape, const2),
                  pl.BlockSpec(wo.shape, const2)],
        out_specs=tok(d),
        out_shape=jax.ShapeDtypeStruct((b, t, d), x.dtype),
        compiler_params=pltpu.CompilerParams(dimension_semantics=("parallel", "parallel"),
                                             vmem_limit_bytes=VMEM_LIMIT),
        name="out_proj",
    )(x, ya, yb, gg, gt, woa, wob, wo)


def _wt_kernel(w_ref, o_ref):
    o_ref[...] = w_ref[...].T.astype(BF16)


def _value_weights_t(w):
    d = w.shape[0]
    blk = 2 * LANES
    va0, vm0, per = COL_VA // blk, COL_VM // blk, DA_WIDTH // blk
    assert COL_VA % blk == 0 and COL_VM % blk == 0 and DA_WIDTH % blk == 0 and ML_WIDTH == DA_WIDTH
    return pl.pallas_call(
        _wt_kernel,
        grid=(2 * per,),
        in_specs=[pl.BlockSpec((d, blk), lambda j: (0, jnp.where(j < per, va0 + j, vm0 + j - per)))],
        out_specs=pl.BlockSpec((blk, d), lambda j: (j, 0)),
        out_shape=jax.ShapeDtypeStruct((2 * per * blk, d), BF16),
        compiler_params=pltpu.CompilerParams(dimension_semantics=("arbitrary",)),
        name="value_weights_t",
    )(w)


def _rope_tables(n_tokens):
    rows = n_tokens // GRID_W
    row_id = jnp.repeat(jnp.arange(rows, dtype=F32), GRID_W)
    col_id = jnp.tile(jnp.arange(GRID_W, dtype=F32), rows)
    n_freq = DA_HD // 4
    inv_freq = ROPE_THETA ** (-jnp.arange(n_freq, dtype=F32) / n_freq)
    ang_r = row_id[:, None] * inv_freq
    ang_c = col_id[:, None] * inv_freq
    cos = jnp.concatenate([jnp.cos(ang_r)] * 2 + [jnp.cos(ang_c)] * 2, axis=-1)
    sin = jnp.concatenate([-jnp.sin(ang_r), jnp.sin(ang_r), -jnp.sin(ang_c), jnp.sin(ang_c)], axis=-1)
    return jnp.tile(cos, (1, LANES // DA_HD)), jnp.tile(sin, (1, LANES // DA_HD))


def kernel(x, c, ctx, c_ctx, norm_w, w_mod, b_mod, w_in, b_if, da_q_norm, da_k_norm, da_lambda_q1, da_lambda_k1, da_lambda_q2, da_lambda_k2, da_head_norm, w_out_a, ml_conv_w, ml_conv_b, ml_wq, ml_wk, ml_head_norm, ml_skip, w_out_b, w_o):
    assert w_mod.shape[0] == 1, "single-layer block"
    b, t, d = x.shape

    rows = -(-(b + 1) // 8) * 8
    cc = jnp.zeros((rows, d), F32).at[:b].set(c).at[b].set(c_ctx)
    mod, lam, bound = _modulation(cc, w_mod[0], b_mod, da_lambda_q1, da_lambda_k1, da_lambda_q2, da_lambda_k2,
                                  da_q_norm, da_k_norm)
    sh_l, sc_l, gt_l = (mod[:b, i * d:(i + 1) * d].reshape(b, 1, d) for i in range(3))
    sh_c, sc_c = (mod[b:b + 1, i * d:(i + 1) * d].reshape(1, 1, d) for i in range(2))

    w = w_in[0]
    w_main = w.astype(BF16)
    w_gates = jnp.pad(w[:, N_MAIN:N_MAIN + ML_GATES], ((0, 0), (0, LANES - ML_GATES)))
    w_tail = jnp.concatenate([w[:, N_MAIN + ML_GATES:], w_gates], axis=1).astype(BF16)
    w_vt = _value_weights_t(w)
    bif = b_if.reshape(ML_GATES, 1)

    grp = np.arange(DA_WIDTH) // DA_HD
    bd = jnp.asarray((grp[:, None] == grp[None, :]).astype(np.float32) / DA_HD, dtype=BF16)
    gq = jnp.tile(da_q_norm, (1, DA_WIDTH // DA_HD))
    gk = jnp.tile(da_k_norm, (1, DA_WIDTH // DA_HD))
    cos, sin = _rope_tables(t)

    n_keys = t + ctx.shape[1]
    qa, k_all, vt_all, za, xm_l, zb, ob, gg, gtl = _in_proj(
        x, sc_l, sh_l, norm_w, w_main, w_tail, w_vt, bif, (bd, gq, gk, cos, sin), latent=True, n_keys=n_keys, key_off=0)
    k_all, vt_all, xm_c, gtc = _in_proj(
        ctx, sc_c, sh_c, norm_w, w_main, w_tail, w_vt, bif, (bd, gk, k_all, vt_all), latent=False, n_keys=n_keys, key_off=t)

    ya = _diff_attn(bound[0, :1], qa, k_all, vt_all, za, lam, da_head_norm)
    yb = _mlstm(xm_l, xm_c, vt_all, gtl, gtc, zb, ob, ml_conv_w[0], ml_conv_b,
                jnp.swapaxes(ml_wq[0], 1, 2).astype(BF16), ml_wk[0].astype(BF16), ml_head_norm, ml_skip)
    return _out_proj(x, ya, yb, gg, gt_l, w_out_a[0].astype(BF16), w_out_b[0].astype(BF16),
                     w_o[0].astype(BF16))
```

```python
import functools

import jax
import jax.numpy as jnp
import numpy as np
from jax import lax
from jax.experimental import pallas as pl
from jax.experimental.pallas import tpu as pltpu

F32 = jnp.float32
BF16 = jnp.bfloat16

EPS = 1e-6
ROPE_THETA = 10000.0
GRID_W = 64
LAM_INIT = 0.8 - 0.6 * 1.0

DA_HEADS = 4
DA_HD = 64
DA_HEAD_W = 2 * DA_HD
DA_WIDTH = DA_HEADS * DA_HEAD_W

ML_HEADS = 4
ML_HD = 128
ML_WIDTH = ML_HEADS * ML_HD
ML_GATES = 4 * ML_HEADS

COL_QA, COL_KA, COL_VA, COL_ZA = (i * DA_WIDTH for i in range(4))
COL_XM, COL_VM, COL_ZB, COL_OB = (4 * DA_WIDTH + i * ML_WIDTH for i in range(4))
N_MAIN = 4 * DA_WIDTH + 4 * ML_WIDTH
COL_GG = N_MAIN + ML_GATES
ML_L = 256
VA_ROWS = ML_HD + 16
LOG2E = 1.4426950408889634
ML_HPP = 2

LANES = 128
VMEM_LIMIT = 56 * 1024 * 1024

IN_TILE = 512
IN_SLAB = 256
OUT_TILE = 1024
ATT_BLK = 256
ATT_KC = 256
ATT_UNROLL = 3
ATT_EXP2_HEADROOM = 30.0
SCORE_BOUND_MARGIN = 1.02
Q_SCALE = DA_HD ** -0.5 * LOG2E


def _sigmoid(x):
    return 0.5 * jnp.tanh(0.5 * x) + 0.5


def _silu(x):
    h = 0.5 * x
    return h * jnp.tanh(h) + h


def _split3(x):
    hi = x.astype(BF16)
    r1 = x - hi.astype(F32)
    mid = r1.astype(BF16)
    lo = (r1 - mid.astype(F32)).astype(BF16)
    return hi, mid, lo


def _mod_kernel(cc_ref, w_ref, b_ref, lq1_ref, lk1_ref, lq2_ref, lk2_ref, gq_ref, gk_ref,
                mod_ref, lam_ref, bound_ref):
    a = _silu(cc_ref[...]).astype(BF16)
    mod_ref[...] = jnp.dot(a, w_ref[...].astype(BF16), preferred_element_type=F32) + b_ref[...]
    s1 = jnp.sum(lq1_ref[...] * lk1_ref[...], axis=-1, keepdims=True)
    s2 = jnp.sum(lq2_ref[...] * lk2_ref[...], axis=-1, keepdims=True)
    lam = jnp.exp(s1) - jnp.exp(s2) + LAM_INIT
    lam_ref[...] = jnp.broadcast_to(lam, lam_ref.shape)
    bq = jnp.max(jnp.abs(gq_ref[...]), axis=-1, keepdims=True)
    bk = jnp.max(jnp.abs(gk_ref[...]), axis=-1, keepdims=True)
    bound_ref[...] = jnp.broadcast_to(SCORE_BOUND_MARGIN * Q_SCALE * DA_HD * bq * bk, bound_ref.shape)


def _modulation(cc, w_mod, b_mod, lq1, lk1, lq2, lk2, gq, gk):
    rows, d = cc.shape
    n3 = w_mod.shape[1]
    vec = pl.BlockSpec((1, DA_HD), lambda j: (0, 0))
    return pl.pallas_call(
        _mod_kernel,
        grid=(n3 // d,),
        in_specs=[pl.BlockSpec((rows, d), lambda j: (0, 0)),
                  pl.BlockSpec((d, d), lambda j: (0, j)),
                  pl.BlockSpec((1, d), lambda j: (0, j)),
                  vec, vec, vec, vec, vec, vec],
        out_specs=[pl.BlockSpec((rows, d), lambda j: (0, j)),
                   pl.BlockSpec((1, LANES), lambda j: (0, 0)),
                   pl.BlockSpec((1, LANES), lambda j: (0, 0))],
        out_shape=[jax.ShapeDtypeStruct((rows, n3), F32),
                   jax.ShapeDtypeStruct((1, LANES), F32),
                   jax.ShapeDtypeStruct((1, LANES), F32)],
        compiler_params=pltpu.CompilerParams(dimension_semantics=("arbitrary",),
                                             vmem_limit_bytes=VMEM_LIMIT),
        name="modulation",
    )(cc, w_mod, b_mod, lq1, lk1, lq2, lk2, gq, gk)


def _group_rms(acc, bd_ref, g_ref):
    ms = jnp.dot((acc * acc).astype(BF16), bd_ref[...], preferred_element_type=F32)
    return acc * lax.rsqrt(ms + EPS) * g_ref[...]


def _rope(x, cos, sin_signed, first_half):
    outs = []
    for c in range(x.shape[1] // LANES):
        xs = x[:, c * LANES:(c + 1) * LANES]
        partner = jnp.where(first_half, pltpu.roll(xs, LANES - 16, 1), pltpu.roll(xs, 16, 1))
        outs.append(xs * cos + partner * sin_signed)
    return jnp.concatenate(outs, axis=1)


def _inproj_kernel(x_ref, sc_ref, sh_ref, nw_ref, w_ref, bif_ref, *rest, latent):
    if latent:
        (bd_ref, gq_ref, gk_ref, cos_ref, sin_ref,
         qa_ref, ka_ref, vt_ref, za_ref, xm_ref, zb_ref, ob_ref, gg_ref, gt_ref) = rest
    else:
        bd_ref, gk_ref, _, _, ka_ref, vt_ref, xm_ref, gt_ref = rest

    nt = (((1,), (1,)), ((), ()))
    slab = min(IN_SLAB, x_ref.shape[1])
    nslab = x_ref.shape[1] // slab
    cps = slab // ATT_KC

    def normed(j):
        xf = x_ref[0, j * slab:(j + 1) * slab, :]
        ms = jnp.mean(xf * xf, axis=-1, keepdims=True)
        y = xf * lax.rsqrt(ms + EPS) * nw_ref[...]
        return (y * (1.0 + sc_ref[0]) + sh_ref[0]).astype(BF16)

    def values(j, h):
        for r0, c0 in ((0, COL_VA), (DA_WIDTH, COL_VM)):
            vt = lax.dot_general(w_ref[c0:c0 + DA_WIDTH, :], h, nt, preferred_element_type=F32).astype(BF16)
            for c in range(cps):
                vt_ref[0, j * cps + c, r0:r0 + DA_WIDTH, :] = vt[:, c * ATT_KC:(c + 1) * ATT_KC]

    def projections(j, h):
        rows = slice(j * slab, (j + 1) * slab)

        def proj(c0, width):
            return lax.dot_general(h, w_ref[c0:c0 + width, :], nt, preferred_element_type=F32)

        if latent:
            lane = lax.broadcasted_iota(jnp.int32, (slab, LANES), 1)
            first_half = (lane % 32) < 16
            cos = cos_ref[rows, :]
            sin = sin_ref[rows, :]
            q = _rope(_group_rms(proj(COL_QA, DA_WIDTH), bd_ref, gq_ref), cos, sin, first_half)
            qa_ref[0, rows, :] = (q * Q_SCALE).astype(BF16)
            k = _rope(_group_rms(proj(COL_KA, DA_WIDTH), bd_ref, gk_ref), cos, sin, first_half)
            ka_ref[0, rows, :] = k.astype(BF16)
            za_ref[0, rows, :] = _silu(proj(COL_ZA, DA_WIDTH)).astype(BF16)
            zb_ref[0, rows, :] = _silu(proj(COL_ZB, ML_WIDTH)).astype(BF16)
            ob_ref[0, rows, :] = _sigmoid(proj(COL_OB, ML_WIDTH)).astype(BF16)
            for c in range(gg_ref.shape[2] // 512):
                gg_ref[0, rows, c * 512:(c + 1) * 512] = _sigmoid(proj(COL_GG + c * 512, 512)).astype(BF16)
        else:
            ka_ref[0, rows, :] = _group_rms(proj(COL_KA, DA_WIDTH), bd_ref, gk_ref).astype(BF16)

        g = proj(N_MAIN, LANES).T[:ML_GATES] + bif_ref[...]
        row = lax.broadcasted_iota(jnp.int32, g.shape, 0)
        logsig = jnp.minimum(g, 0.0) - jnp.log(1.0 + jnp.exp(-jnp.abs(g)))
        gt_ref[0, :, rows] = jnp.where((row // ML_HEADS) % 2 == 1, logsig, g)

        xm_ref[0, rows, :] = proj(COL_XM, ML_WIDTH).astype(BF16)

    h = normed(0)
    for j in range(nslab):
        values(j, h)
        h_next = normed(j + 1) if j + 1 < nslab else None
        projections(j, h)
        h = h_next


def _in_proj(x, sc, sh, norm_w, w_t, bif, extra, *, latent, n_keys, key_off):
    b, t, d = x.shape
    tm = min(IN_TILE, t)
    assert key_off % tm == 0
    kblk = key_off // tm
    grid = (b, t // tm)
    const2 = lambda bi, i: (0, 0)
    mod_map = (lambda bi, i: (bi, 0, 0)) if latent else (lambda bi, i: (0, 0, 0))
    tok = lambda width, dtype: (pl.BlockSpec((1, tm, width), lambda bi, i: (bi, i, 0)),
                                jax.ShapeDtypeStruct((b, t, width), dtype))
    keys = (pl.BlockSpec((1, tm, DA_WIDTH), lambda bi, i: (bi, kblk + i, 0)),
            jax.ShapeDtypeStruct((b, n_keys, DA_WIDTH), BF16))
    cpt = tm // ATT_KC
    n_val = DA_WIDTH + ML_WIDTH
    vals = (pl.BlockSpec((1, cpt, n_val, ATT_KC), lambda bi, i: (bi, kblk + i, 0, 0)),
            jax.ShapeDtypeStruct((b, n_keys // ATT_KC, n_val, ATT_KC), BF16))
    in_specs = [pl.BlockSpec((1, tm, d), lambda bi, i: (bi, i, 0)),
                pl.BlockSpec((1, 1, d), mod_map),
                pl.BlockSpec((1, 1, d), mod_map),
                pl.BlockSpec((1, d), const2),
                pl.BlockSpec(w_t.shape, const2),
                pl.BlockSpec(bif.shape, const2)]
    if latent:
        bd, gq, gk, cos, sin = extra
        in_specs += [pl.BlockSpec(bd.shape, const2), pl.BlockSpec(gq.shape, const2),
                     pl.BlockSpec(gk.shape, const2),
                     pl.BlockSpec((tm, LANES), lambda bi, i: (i, 0)),
                     pl.BlockSpec((tm, LANES), lambda bi, i: (i, 0))]
        outs = ([tok(DA_WIDTH, BF16), keys, vals, tok(DA_WIDTH, BF16)] + [tok(ML_WIDTH, BF16)] * 3
                + [tok(2 * d, BF16)])
        aliases = {}
    else:
        bd, gk, _, _ = extra
        in_specs += [pl.BlockSpec(bd.shape, const2), pl.BlockSpec(gk.shape, const2),
                     pl.BlockSpec(memory_space=pl.ANY), pl.BlockSpec(memory_space=pl.ANY)]
        outs = [keys, vals, tok(ML_WIDTH, BF16)]
        aliases = {len(in_specs) - 2: 0, len(in_specs) - 1: 1}
    outs.append((pl.BlockSpec((1, ML_GATES, tm), lambda bi, i: (bi, 0, i)),
                 jax.ShapeDtypeStruct((b, ML_GATES, t), F32)))
    return pl.pallas_call(
        functools.partial(_inproj_kernel, latent=latent),
        grid=grid,
        in_specs=in_specs,
        out_specs=[o[0] for o in outs],
        out_shape=[o[1] for o in outs],
        input_output_aliases=aliases,
        compiler_params=pltpu.CompilerParams(dimension_semantics=("parallel", "parallel"),
                                             vmem_limit_bytes=VMEM_LIMIT),
        name="in_proj_latent" if latent else "in_proj_context",
    )(x, sc, sh, norm_w, w_t, bif, *extra)


def _attn_kernel(bound_ref, q_ref, k_ref, vt_ref, z_ref, lam_ref, hg_ref, o_ref, s_scr, e_scr, acc_scr, qm_scr):
    nblk = q_ref.shape[1] // ATT_BLK
    nkc = k_ref.shape[1] // ATT_KC
    sub = ATT_KC // 8
    lam = lam_ref[:, 0:1]
    nt = (((1,), (1,)), ((), ()))
    neg = jnp.full((8, ATT_BLK), -jnp.inf, F32)
    zero8 = jnp.zeros((8, ATT_BLK), F32)
    bound = bound_ref[0]

    def mask_queries(j):
        q = q_ref[0, j * ATT_BLK:(j + 1) * ATT_BLK, :]
        lane = lax.broadcasted_iota(jnp.int32, q.shape, 1)
        zero = jnp.zeros_like(q)
        qm_scr[0] = jnp.where(lane < DA_HD, q, zero)
        qm_scr[1] = jnp.where(lane >= DA_HD, q, zero)

    def scores(c, mp):
        kc = k_ref[0, pl.ds(pl.multiple_of(c * ATT_KC, ATT_KC), ATT_KC), :]
        return lax.dot_general(kc, qm_scr[mp], nt, preferred_element_type=F32).reshape(sub, 8, ATT_BLK)

    def exps(c, par, mp, st, stab8, lacc):
        e = jnp.exp2(st - stab8[None])
        e_scr[par, mp, c] = e.reshape(ATT_KC, ATT_BLK)
        return lacc + jnp.sum(e, axis=0)

    def values(c, par, r8):
        e1 = e_scr[par, 0, c].reshape(sub, 8, ATT_BLK)
        e2 = e_scr[par, 1, c].reshape(sub, 8, ATT_BLK)
        pt = (e1 - e2 * r8[None]).reshape(ATT_KC, ATT_BLK).astype(BF16)
        acc_scr[...] += jnp.dot(vt_ref[0, c], pt, preferred_element_type=F32)

    def finish(j, inv_l1):
        o = (acc_scr[...] * inv_l1).T
        ms = jnp.mean(o * o, axis=-1, keepdims=True)
        on = o * lax.rsqrt(ms + EPS) * hg_ref[...]
        rows = slice(j * ATT_BLK, (j + 1) * ATT_BLK)
        o_ref[0, rows, :] = (on * (1.0 - LAM_INIT) * z_ref[0, rows, :].astype(F32)).astype(BF16)

    def pipeline(exact_max):
        d_exp = 1 if exact_max else 0
        d_val = d_exp + 1
        bound8 = jnp.full((8, ATT_BLK), bound, F32)
        m8, r8, inv_l1 = {}, {}, {}
        for j in range(nblk + d_val):
            do_s, do_e, do_v = j < nblk, d_exp <= j < nblk + d_exp, d_val <= j
            if do_s:
                mask_queries(j)
            if do_v:
                acc_scr[...] = jnp.zeros_like(acc_scr)

            def body(c, carry, j=j, do_s=do_s, do_e=do_e, do_v=do_v):
                macc, lacc = carry
                if exact_max:
                    if do_s:
                        macc = list(macc)
                        for mp in range(2):
                            st = scores(c, mp)
                            s_scr[j % 2, mp, c] = st.reshape(ATT_KC, ATT_BLK)
                            macc[mp] = jnp.maximum(macc[mp], jnp.max(st, axis=0))
                        macc = tuple(macc)
                    if do_e:
                        jb = j - d_exp
                        lacc = tuple(
                            exps(c, jb % 2, mp, s_scr[jb % 2, mp, c].reshape(sub, 8, ATT_BLK), m8[jb][mp], lacc[mp])
                            for mp in range(2))
                elif do_s:
                    lacc = tuple(exps(c, j % 2, mp, scores(c, mp), bound8, lacc[mp]) for mp in range(2))
                if do_v:
                    values(c, (j - d_val) % 2, r8[j - d_val])
                return macc, lacc

            macc, lacc = lax.fori_loop(0, nkc, body, ((neg, neg), (zero8, zero8)),
                                       unroll=ATT_UNROLL if exact_max else nkc)
            if do_v:
                finish(j - d_val, inv_l1.pop(j - d_val))
                r8.pop(j - d_val)
            if do_e:
                l1 = jnp.sum(lacc[0], axis=0, keepdims=True)
                l2 = jnp.sum(lacc[1], axis=0, keepdims=True)
                r8[j - d_exp] = jnp.broadcast_to(lam * l1 / l2, (8, ATT_BLK))
                inv_l1[j - d_exp] = 1.0 / l1
                m8.pop(j - d_exp, None)
            if exact_max and do_s:
                m8[j] = tuple(jnp.broadcast_to(jnp.max(a, axis=0, keepdims=True), (8, ATT_BLK)) for a in macc)

    use_bound = bound <= 0.5 * (126.0 - ATT_EXP2_HEADROOM)

    @pl.when(use_bound)
    def _():
        pipeline(False)

    @pl.when(jnp.logical_not(use_bound))
    def _():
        pipeline(True)


def _diff_attn(bound, qa, k_all, vt_all, za, lam, head_g):
    b, t, _ = qa.shape
    n_keys = k_all.shape[1]
    nkc = n_keys // ATT_KC
    assert t % ATT_BLK == 0 and n_keys % ATT_KC == 0 and nkc % ATT_UNROLL == 0
    qmap = lambda bi, h: (bi, 0, h)
    const2 = lambda bi, h: (0, 0)
    return pl.pallas_call(
        _attn_kernel,
        grid=(b, DA_HEADS),
        in_specs=[pl.BlockSpec(memory_space=pltpu.SMEM),
                  pl.BlockSpec((1, t, DA_HEAD_W), qmap),
                  pl.BlockSpec((1, n_keys, DA_HEAD_W), qmap),
                  pl.BlockSpec((1, nkc, DA_HEAD_W, ATT_KC), lambda bi, h: (bi, 0, h, 0)),
                  pl.BlockSpec((1, t, DA_HEAD_W), qmap),
                  pl.BlockSpec((1, LANES), const2),
                  pl.BlockSpec((1, DA_HEAD_W), const2)],
        out_specs=pl.BlockSpec((1, t, DA_HEAD_W), qmap),
        out_shape=jax.ShapeDtypeStruct((b, t, DA_WIDTH), BF16),
        scratch_shapes=[pltpu.VMEM((2, 2, nkc, ATT_KC, ATT_BLK), F32),
                        pltpu.VMEM((2, 2, nkc, ATT_KC, ATT_BLK), F32),
                        pltpu.VMEM((DA_HEAD_W, ATT_BLK), F32),
                        pltpu.VMEM((2, ATT_BLK, DA_HEAD_W), BF16)],
        compiler_params=pltpu.CompilerParams(dimension_semantics=("parallel", "arbitrary"),
                                             vmem_limit_bytes=VMEM_LIMIT),
        name="diff_attn",
    )(bound, qa, k_all, vt_all, za, lam, head_g)


def _conv_silu(x, w, b, pad_s):
    t = x.shape[0]
    edge = jnp.zeros((8, x.shape[1]), F32)
    pad_s[0:8, :] = edge
    pad_s[8:8 + t, :] = x.astype(F32)
    pad_s[8 + t:16 + t, :] = edge
    y = b + w[0:1, :] * pad_s[7:7 + t, :]
    y = y + w[1:2, :] * pad_s[8:8 + t, :]
    y = y + w[2:3, :] * pad_s[9:9 + t, :]
    return _silu(y)


def _rows_to_cols(rows, eye):
    nt = (((1,), (1,)), ((), ()))
    out = None
    for part in _split3(rows):
        term = lax.dot_general(eye, part, nt, preferred_element_type=F32)
        out = term if out is None else out + term
    return out


def _cumsum_rows(rows, tri):
    out = None
    for part in _split3(rows):
        term = jnp.dot(part, tri, preferred_element_type=F32)
        out = term if out is None else out + term
    return out


def _cummax_rows(rows, reverse):
    n = rows.shape[1]
    lane = lax.broadcasted_iota(jnp.int32, rows.shape, 1)
    sh = 1
    while sh < n:
        if reverse:
            shifted = jnp.where(lane < n - sh, pltpu.roll(rows, n - sh, 1), -jnp.inf)
        else:
            shifted = jnp.where(lane >= sh, pltpu.roll(rows, sh, 1), -jnp.inf)
        rows = jnp.maximum(rows, shifted)
        sh *= 2
    return rows


def _mlstm_kernel(xl_ref, xc_ref, vt_ref, gl_ref, gc_ref, zb_ref, ob_ref,
                  cw_ref, cb_ref, wqt_ref, wk_ref, hg_ref, sk_ref, o_ref,
                  xcv_s, qt_s, k_s, vat_s, hf_s, hb_s, upd_s, stb_s, r1_s, bcol_s, pad_s, xb_s):
    hp = pl.program_id(1)
    n_lat = xl_ref.shape[1]
    nchunk = n_lat // ML_L
    L = ML_L
    nt = (((1,), (1,)), ((), ()))

    r_i = lax.broadcasted_iota(jnp.int32, (L, L), 0)
    c_i = lax.broadcasted_iota(jnp.int32, (L, L), 1)
    eye = jnp.where(r_i == c_i, 1.0, 0.0).astype(BF16)
    tris = (jnp.where(r_i <= c_i, 1.0, 0.0).astype(BF16), jnp.where(r_i >= c_i, 1.0, 0.0).astype(BF16))
    visible = (r_i <= c_i, r_i >= c_i)
    row = lax.broadcasted_iota(jnp.int32, (VA_ROWS - ML_HD, L), 0)
    ones_row = jnp.where(row == 0, 1.0, 0.0).astype(BF16)
    lane8 = lax.broadcasted_iota(jnp.int32, (8, L), 1)
    zero11 = jnp.zeros((1, 1), F32)

    gates = {}
    per_slot = {}

    def prepare(hh, x_ref, slot0):
        cols_h = slice(hh * ML_HD, (hh + 1) * ML_HD)
        n = x_ref.shape[1]
        xc = _conv_silu(x_ref[0, :, cols_h], cw_ref[:, cols_h], cb_ref[:, cols_h], pad_s)
        xb_s[0:n, :] = xc.astype(BF16)
        xb = xb_s[0:n, :]
        qt = lax.dot_general(wqt_ref[hh], xb, nt, preferred_element_type=F32).astype(BF16)
        for j in range(n // L):
            qt_s[hh, slot0 + j] = qt[:, j * L:(j + 1) * L]
        k = jnp.dot(xb, wk_ref[hh], preferred_element_type=F32) * (ML_HD ** -0.5)
        k_s[hh, slot0 * L:slot0 * L + n, :] = k.astype(BF16)
        return xc

    def gate_algebra(hh, d, g_ref, r0):
        head = hp * ML_HPP + hh
        ig = g_ref[0, (2 * d) * ML_HEADS + head] * LOG2E
        lf = g_ref[0, (2 * d + 1) * ML_HEADS + head] * LOG2E
        n = ig.shape[0]
        if n == 1:
            ig = jnp.broadcast_to(ig, (8, L))
            lf = jnp.broadcast_to(lf, (8, L))
        a = _cumsum_rows(lf, tris[d])
        bb = ig - a
        pick = (lane8 == L - 1) if d == 0 else (lane8 == 0)
        f_tot = jnp.sum(jnp.where(pick, a, 0.0), axis=-1, keepdims=True)
        b_max = jnp.max(bb, axis=-1, keepdims=True)
        cmax = _cummax_rows(bb, d == 1)
        cols = _rows_to_cols(bb, eye)
        for j in range(n):
            bcol_s[hh * 2 + d, r0 + j] = jnp.broadcast_to(cols[:, j:j + 1], (L, ML_HD))
            per_slot[hh, d, r0 + j] = (a[j:j + 1], cmax[j:j + 1], f_tot[j:j + 1], b_max[j:j + 1])

    for hh in range(ML_HPP):
        gate_algebra(hh, 0, gl_ref, 0)
        prepare(hh, xc_ref, nchunk)
        gate_algebra(hh, 0, gc_ref, nchunk)
        gate_algebra(hh, 1, gl_ref, 0)
        xcv_s[hh] = prepare(hh, xl_ref, 0)
        gate_algebra(hh, 1, gc_ref, nchunk)
        cols_h = slice(hh * ML_HD, (hh + 1) * ML_HD)
        for j in range(nchunk + 1):
            vat_s[hh, j, 0:ML_HD, :] = vt_ref[0, j, cols_h, :]
            vat_s[hh, j, ML_HD:VA_ROWS, :] = ones_row
        for d in range(2):
            m = zero11
            order = [nchunk] + (list(range(nchunk)) if d == 0 else list(range(nchunk - 1, -1, -1)))
            for slot in order:
                a_row, cmax_row, f_tot, b_max = per_slot[hh, d, slot]
                m_new = jnp.maximum(f_tot + m, f_tot + b_max)
                gates[hh, d, slot] = (a_row, cmax_row, f_tot, m, m_new)
                m = m_new

    latent = [(hh, d, c) for hh in range(ML_HPP) for d in range(2) for c in range(nchunk)]
    every = [(hh, d, nchunk) for hh in range(ML_HPP) for d in range(2)] + latent

    def scores(item):
        hh, _, c = item
        return jnp.dot(k_s[hh, c * L:(c + 1) * L, :], qt_s[hh, c], preferred_element_type=F32)

    def decayed(item, st):
        hh, d, c = item
        _, cmax_row, _, m_old, _ = gates[item]
        mm = jnp.maximum(cmax_row, m_old)
        bcol = bcol_s[hh * 2 + d, c]
        logd = jnp.where(visible[d], jnp.concatenate([bcol] * (L // ML_HD), axis=1) - mm, -jnp.inf)
        return (st * jnp.exp2(logd)).astype(BF16)

    def increment(item):
        hh, d, slot = item
        _, _, f_tot, _, m_new = gates[item]
        kc = k_s[hh, slot * L:(slot + 1) * L, :]
        kw = kc * jnp.exp2(bcol_s[hh * 2 + d, slot] + (f_tot - m_new)).astype(BF16)
        upd_s[hh * 2 + d, slot] = jnp.dot(vat_s[hh, slot], kw, preferred_element_type=F32)

    n_lat = len(latent)
    s_val, p_val = {}, {}
    for t in range(n_lat + 2):
        if t < n_lat:
            s_val[t] = scores(latent[t])
        if 0 <= t - 1 < n_lat:
            p_val[t - 1] = decayed(latent[t - 1], s_val.pop(t - 1))
        if 0 <= t - 2 < n_lat:
            hh, d, c = latent[t - 2]
            r1_s[hh * 2 + d, c] = jnp.dot(vat_s[hh, c], p_val.pop(t - 2), preferred_element_type=F32)
        if t < len(every):
            increment(every[t])
    for t in range(n_lat + 2, len(every)):
        increment(every[t])

    for hh in range(ML_HPP):
        for d in range(2):
            sd = hh * 2 + d
            st = upd_s[sd, nchunk]
            for c in (range(nchunk) if d == 0 else range(nchunk - 1, -1, -1)):
                _, _, f_tot, m_old, m_new = gates[hh, d, c]
                stb_s[sd, c] = st.astype(BF16)
                st = jnp.exp2(f_tot + m_old - m_new) * st + upd_s[sd, c]

    def carried(item):
        hh, d, c = item
        return jnp.dot(stb_s[hh * 2 + d, c], qt_s[hh, c], preferred_element_type=F32)

    def emit(item, r2):
        hh, d, c = item
        a_row, cmax_row, _, m_old, _ = gates[item]
        mm = jnp.maximum(cmax_row, m_old)
        tot = r1_s[hh * 2 + d, c] + jnp.exp2(m_old - mm) * r2
        den = tot[ML_HD:ML_HD + 1, :]
        floor = jnp.exp2(-(a_row + mm))
        (hf_s if d == 0 else hb_s)[hh, c] = tot[0:ML_HD, :] * (1.0 / jnp.maximum(jnp.abs(den), floor))

    r2_val = {}
    for t in range(n_lat + 1):
        if t < n_lat:
            r2_val[t] = carried(latent[t])
        if t >= 1:
            emit(latent[t - 1], r2_val.pop(t - 1))


    for hh in range(ML_HPP):
        cols_h = slice(hh * ML_HD, (hh + 1) * ML_HD)
        for c in range(nchunk):
            ht = hf_s[hh, c] + hb_s[hh, c]
            ms = jnp.mean(ht * ht, axis=0, keepdims=True)
            hn = (ht * lax.rsqrt(ms + EPS)).T * hg_ref[:, cols_h]
            rows = slice(c * L, (c + 1) * L)
            y = ((ob_ref[0, rows, cols_h].astype(F32) * hn + sk_ref[:, cols_h] * xcv_s[hh, rows, :])
                 * zb_ref[0, rows, cols_h].astype(F32))
            o_ref[0, rows, cols_h] = y.astype(BF16)


def _mlstm(xm_l, xm_c, vt_all, gt_l, gt_c, zb, ob, conv_w, conv_b, wqt, wk, head_g, skip):
    b, t, _ = xm_l.shape
    n_ctx = xm_c.shape[1]
    nchunk = t // ML_L
    nslot = nchunk + 1
    assert t % (8 * ML_L) == 0 and n_ctx == ML_L and vt_all.shape[1] == nslot
    gl = gt_l.reshape(b, ML_GATES, nchunk, ML_L)
    gc = gt_c.reshape(b, ML_GATES, 1, ML_L)
    pw = ML_HPP * ML_HD
    vt_blk0 = DA_WIDTH // pw
    tokmap = lambda bi, h: (bi, 0, h)
    gmap = lambda bi, h: (bi, 0, 0, 0)
    hvec = lambda bi, h: (0, h)
    wmap = lambda bi, h: (h, 0, 0)
    return pl.pallas_call(
        _mlstm_kernel,
        grid=(b, ML_HEADS // ML_HPP),
        in_specs=[pl.BlockSpec((1, t, pw), tokmap),
                  pl.BlockSpec((1, n_ctx, pw), tokmap),
                  pl.BlockSpec((1, nslot, pw, ML_L), lambda bi, h: (bi, 0, vt_blk0 + h, 0)),
                  pl.BlockSpec((1, ML_GATES, nchunk, ML_L), gmap),
                  pl.BlockSpec((1, ML_GATES, 1, ML_L), gmap),
                  pl.BlockSpec((1, t, pw), tokmap),
                  pl.BlockSpec((1, t, pw), tokmap),
                  pl.BlockSpec((conv_w.shape[0], pw), hvec),
                  pl.BlockSpec((1, pw), hvec),
                  pl.BlockSpec((ML_HPP, ML_HD, ML_HD), wmap),
                  pl.BlockSpec((ML_HPP, ML_HD, ML_HD), wmap),
                  pl.BlockSpec((1, pw), hvec),
                  pl.BlockSpec((1, pw), hvec)],
        out_specs=pl.BlockSpec((1, t, pw), tokmap),
        out_shape=jax.ShapeDtypeStruct((b, t, ML_WIDTH), BF16),
        scratch_shapes=[pltpu.VMEM((ML_HPP, t, ML_HD), F32),
                        pltpu.VMEM((ML_HPP, nslot, ML_HD, ML_L), BF16),
                        pltpu.VMEM((ML_HPP, nslot * ML_L, ML_HD), BF16),
                        pltpu.VMEM((ML_HPP, nslot, VA_ROWS, ML_L), BF16),
                        pltpu.VMEM((ML_HPP, nchunk, ML_HD, ML_L), F32),
                        pltpu.VMEM((ML_HPP, nchunk, ML_HD, ML_L), F32),
                        pltpu.VMEM((ML_HPP * 2, nslot, VA_ROWS, ML_HD), F32),
                        pltpu.VMEM((ML_HPP * 2, nchunk, VA_ROWS, ML_HD), BF16),
                        pltpu.VMEM((ML_HPP * 2, nchunk, VA_ROWS, ML_L), F32),
                        pltpu.VMEM((ML_HPP * 2, nslot, ML_L, ML_HD), F32),
                        pltpu.VMEM((t + 16, ML_HD), F32),
                        pltpu.VMEM((t, ML_HD), BF16)],
        compiler_params=pltpu.CompilerParams(dimension_semantics=("parallel", "arbitrary"),
                                             vmem_limit_bytes=VMEM_LIMIT),
        name="mlstm",
    )(xm_l, xm_c, vt_all, gl, gc, zb, ob, conv_w, conv_b, wqt, wk, head_g, skip)


def _outproj_kernel(x_ref, ya_ref, yb_ref, gg_ref, gt_ref, woa_ref, wob_ref, wo_ref, o_ref):
    d = x_ref.shape[2]
    a = jnp.dot(ya_ref[0], woa_ref[...], preferred_element_type=F32)
    b = jnp.dot(yb_ref[0], wob_ref[...], preferred_element_type=F32)
    y = gg_ref[0, :, 0:d].astype(F32) * a + gg_ref[0, :, d:2 * d].astype(F32) * b
    o = jnp.dot(y.astype(BF16), wo_ref[...], preferred_element_type=F32)
    o_ref[0] = x_ref[0] + gt_ref[0] * o


def _out_proj(x, ya, yb, gg, gt, woa, wob, wo):
    b, t, d = x.shape
    tm = min(OUT_TILE, t)
    tok = lambda width: pl.BlockSpec((1, tm, width), lambda bi, i: (bi, i, 0))
    const2 = lambda bi, i: (0, 0)
    return pl.pallas_call(
        _outproj_kernel,
        grid=(b, t // tm),
        in_specs=[tok(d), tok(DA_WIDTH), tok(ML_WIDTH), tok(2 * d),
                  pl.BlockSpec((1, 1, d), lambda bi, i: (bi, 0, 0)),
                  pl.BlockSpec(woa.shape, const2), pl.BlockSpec(wob.shape, const2),
                  pl.BlockSpec(wo.shape, const2)],
        out_specs=tok(d),
        out_shape=jax.ShapeDtypeStruct((b, t, d), x.dtype),
        compiler_params=pltpu.CompilerParams(dimension_semantics=("parallel", "parallel"),
                                             vmem_limit_bytes=VMEM_LIMIT),
        name="out_proj",
    )(x, ya, yb, gg, gt, woa, wob, wo)


def _rope_tables(n_tokens):
    rows = n_tokens // GRID_W
    row_id = jnp.repeat(jnp.arange(rows, dtype=F32), GRID_W)
    col_id = jnp.tile(jnp.arange(GRID_W, dtype=F32), rows)
    n_freq = DA_HD // 4
    inv_freq = ROPE_THETA ** (-jnp.arange(n_freq, dtype=F32) / n_freq)
    ang_r = row_id[:, None] * inv_freq
    ang_c = col_id[:, None] * inv_freq
    cos = jnp.concatenate([jnp.cos(ang_r)] * 2 + [jnp.cos(ang_c)] * 2, axis=-1)
    sin = jnp.concatenate([-jnp.sin(ang_r), jnp.sin(ang_r), -jnp.sin(ang_c), jnp.sin(ang_c)], axis=-1)
    return jnp.tile(cos, (1, LANES // DA_HD)), jnp.tile(sin, (1, LANES // DA_HD))


def kernel(x, c, ctx, c_ctx, norm_w, w_mod, b_mod, w_in, b_if, da_q_norm, da_k_norm, da_lambda_q1, da_lambda_k1, da_lambda_q2, da_lambda_k2, da_head_norm, w_out_a, ml_conv_w, ml_conv_b, ml_wq, ml_wk, ml_head_norm, ml_skip, w_out_b, w_o):
    assert w_mod.shape[0] == 1, "single-layer block"
    b, t, d = x.shape

    rows = -(-(b + 1) // 8) * 8
    cc = jnp.zeros((rows, d), F32).at[:b].set(c).at[b].set(c_ctx)
    mod, lam, bound = _modulation(cc, w_mod[0], b_mod, da_lambda_q1, da_lambda_k1, da_lambda_q2, da_lambda_k2,
                                  da_q_norm, da_k_norm)
    sh_l, sc_l, gt_l = (mod[:b, i * d:(i + 1) * d].reshape(b, 1, d) for i in range(3))
    sh_c, sc_c = (mod[b:b + 1, i * d:(i + 1) * d].reshape(1, 1, d) for i in range(2))

    w_t = jnp.swapaxes(w_in[0], 0, 1).astype(BF16)
    bif = b_if.reshape(ML_GATES, 1)

    grp = np.arange(DA_WIDTH) // DA_HD
    bd = jnp.asarray((grp[:, None] == grp[None, :]).astype(np.float32) / DA_HD, dtype=BF16)
    gq = jnp.tile(da_q_norm, (1, DA_WIDTH // DA_HD))
    gk = jnp.tile(da_k_norm, (1, DA_WIDTH // DA_HD))
    cos, sin = _rope_tables(t)

    n_keys = t + ctx.shape[1]
    qa, k_all, vt_all, za, xm_l, zb, ob, gg, gtl = _in_proj(
        x, sc_l, sh_l, norm_w, w_t, bif, (bd, gq, gk, cos, sin), latent=True, n_keys=n_keys, key_off=0)
    k_all, vt_all, xm_c, gtc = _in_proj(
        ctx, sc_c, sh_c, norm_w, w_t, bif, (bd, gk, k_all, vt_all), latent=False, n_keys=n_keys, key_off=t)

    ya = _diff_attn(bound[0, :1], qa, k_all, vt_all, za, lam, da_head_norm)
    yb = _mlstm(xm_l, xm_c, vt_all, gtl, gtc, zb, ob, ml_conv_w[0], ml_conv_b,
                jnp.swapaxes(ml_wq[0], 1, 2).astype(BF16), ml_wk[0].astype(BF16), ml_head_norm, ml_skip)
    return _out_proj(x, ya, yb, gg, gt_l, w_out_a[0].astype(BF16), w_out_b[0].astype(BF16),
                     w_o[0].astype(BF16))
```

```python
import functools

import jax
import jax.numpy as jnp
import numpy as np
from jax import lax
from jax.experimental import pallas as pl
from jax.experimental.pallas import tpu as pltpu

F32 = jnp.float32
BF16 = jnp.bfloat16

EPS = 1e-6
ROPE_THETA = 10000.0
GRID_W = 64
LAM_INIT = 0.8 - 0.6 * 1.0

DA_HEADS = 4
DA_HD = 64
DA_HEAD_W = 2 * DA_HD
DA_WIDTH = DA_HEADS * DA_HEAD_W

ML_HEADS = 4
ML_HD = 128
ML_WIDTH = ML_HEADS * ML_HD
ML_GATES = 4 * ML_HEADS

COL_QA, COL_KA, COL_VA, COL_ZA = (i * DA_WIDTH for i in range(4))
COL_XM, COL_VM, COL_ZB, COL_OB = (4 * DA_WIDTH + i * ML_WIDTH for i in range(4))
N_MAIN = 4 * DA_WIDTH + 4 * ML_WIDTH
COL_GG = N_MAIN + ML_GATES
ML_L = 256
VA_ROWS = ML_HD + 16
LOG2E = 1.4426950408889634
ML_HPP = 2

LANES = 128
VMEM_LIMIT = 56 * 1024 * 1024

IN_TILE = 512
IN_SLAB = 512
OUT_TILE = 1024
ATT_BLK = 256
ATT_KC = 256
ATT_UNROLL = 3
ATT_EXP2_HEADROOM = 30.0
SCORE_BOUND_MARGIN = 1.02
Q_SCALE = DA_HD ** -0.5 * LOG2E


def _sigmoid(x):
    return 0.5 * jnp.tanh(0.5 * x) + 0.5


def _silu(x):
    h = 0.5 * x
    return h * jnp.tanh(h) + h


def _split3(x):
    hi = x.astype(BF16)
    r1 = x - hi.astype(F32)
    mid = r1.astype(BF16)
    lo = (r1 - mid.astype(F32)).astype(BF16)
    return hi, mid, lo


def _mod_kernel(cc_ref, w_ref, b_ref, lq1_ref, lk1_ref, lq2_ref, lk2_ref, gq_ref, gk_ref,
                mod_ref, lam_ref, bound_ref):
    a = _silu(cc_ref[...]).astype(BF16)
    mod_ref[...] = jnp.dot(a, w_ref[...].astype(BF16), preferred_element_type=F32) + b_ref[...]
    s1 = jnp.sum(lq1_ref[...] * lk1_ref[...], axis=-1, keepdims=True)
    s2 = jnp.sum(lq2_ref[...] * lk2_ref[...], axis=-1, keepdims=True)
    lam = jnp.exp(s1) - jnp.exp(s2) + LAM_INIT
    lam_ref[...] = jnp.broadcast_to(lam, lam_ref.shape)
    bq = jnp.max(jnp.abs(gq_ref[...]), axis=-1, keepdims=True)
    bk = jnp.max(jnp.abs(gk_ref[...]), axis=-1, keepdims=True)
    bound_ref[...] = jnp.broadcast_to(SCORE_BOUND_MARGIN * Q_SCALE * DA_HD * bq * bk, bound_ref.shape)


def _modulation(cc, w_mod, b_mod, lq1, lk1, lq2, lk2, gq, gk):
    rows, d = cc.shape
    n3 = w_mod.shape[1]
    vec = pl.BlockSpec((1, DA_HD), lambda j: (0, 0))
    return pl.pallas_call(
        _mod_kernel,
        grid=(n3 // d,),
        in_specs=[pl.BlockSpec((rows, d), lambda j: (0, 0)),
                  pl.BlockSpec((d, d), lambda j: (0, j)),
                  pl.BlockSpec((1, d), lambda j: (0, j)),
                  vec, vec, vec, vec, vec, vec],
        out_specs=[pl.BlockSpec((rows, d), lambda j: (0, j)),
                   pl.BlockSpec((1, LANES), lambda j: (0, 0)),
                   pl.BlockSpec((1, LANES), lambda j: (0, 0))],
        out_shape=[jax.ShapeDtypeStruct((rows, n3), F32),
                   jax.ShapeDtypeStruct((1, LANES), F32),
                   jax.ShapeDtypeStruct((1, LANES), F32)],
        compiler_params=pltpu.CompilerParams(dimension_semantics=("arbitrary",),
                                             vmem_limit_bytes=VMEM_LIMIT),
        name="modulation",
    )(cc, w_mod, b_mod, lq1, lk1, lq2, lk2, gq, gk)


def _group_rms(acc, bd_ref, g_ref):
    sq = (acc * acc).astype(BF16)
    n = bd_ref.shape[0]
    ms = jnp.concatenate([jnp.dot(sq[:, c:c + n], bd_ref[...], preferred_element_type=F32)
                          for c in range(0, acc.shape[1], n)], axis=1)
    return acc * lax.rsqrt(ms + EPS) * g_ref[...]


def _rope(x, cos, sin_signed, first_half):
    outs = []
    for c in range(x.shape[1] // LANES):
        xs = x[:, c * LANES:(c + 1) * LANES]
        partner = jnp.where(first_half, pltpu.roll(xs, LANES - 16, 1), pltpu.roll(xs, 16, 1))
        outs.append(xs * cos + partner * sin_signed)
    return jnp.concatenate(outs, axis=1)


def _inproj_kernel(x_ref, sc_ref, sh_ref, nw_ref, w_ref, bif_ref, *rest, latent):
    if latent:
        (bd_ref, gq_ref, gk_ref, cos_ref, sin_ref,
         qa_ref, ka_ref, vt_ref, za_ref, xm_ref, zb_ref, ob_ref, gt_ref) = rest
    else:
        bd_ref, gk_ref, _, _, ka_ref, vt_ref, xm_ref, gt_ref = rest

    nt = (((1,), (1,)), ((), ()))
    slab = min(IN_SLAB, x_ref.shape[1])
    nslab = x_ref.shape[1] // slab
    cps = slab // ATT_KC

    def normed(c):
        xf = x_ref[0, c * ATT_KC:(c + 1) * ATT_KC, :]
        ms = jnp.mean(xf * xf, axis=-1, keepdims=True)
        y = xf * lax.rsqrt(ms + EPS) * nw_ref[...]
        return (y * (1.0 + sc_ref[0]) + sh_ref[0]).astype(BF16)

    def values(c, hc):
        for r0, c0 in ((0, COL_VA), (DA_WIDTH, COL_VM)):
            vt_ref[0, c, r0:r0 + DA_WIDTH, :] = lax.dot_general(
                w_ref[c0:c0 + DA_WIDTH, :], hc, nt, preferred_element_type=F32).astype(BF16)

    def projections(j, h):
        rows = slice(j * slab, (j + 1) * slab)

        def proj(c0, width):
            return lax.dot_general(h, w_ref[c0:c0 + width, :], nt, preferred_element_type=F32)

        if latent:
            lane = lax.broadcasted_iota(jnp.int32, (slab, LANES), 1)
            first_half = (lane % 32) < 16
            cos = cos_ref[rows, :]
            sin = sin_ref[rows, :]
            q = _rope(_group_rms(proj(COL_QA, DA_WIDTH), bd_ref, gq_ref), cos, sin, first_half)
            qa_ref[0, rows, :] = (q * Q_SCALE).astype(BF16)
            k = _rope(_group_rms(proj(COL_KA, DA_WIDTH), bd_ref, gk_ref), cos, sin, first_half)
            ka_ref[0, rows, :] = k.astype(BF16)
            za_ref[0, rows, :] = _silu(proj(COL_ZA, DA_WIDTH)).astype(BF16)
            zb_ref[0, rows, :] = _silu(proj(COL_ZB, ML_WIDTH)).astype(BF16)
            ob_ref[0, rows, :] = _sigmoid(proj(COL_OB, ML_WIDTH)).astype(BF16)
        else:
            ka_ref[0, rows, :] = _group_rms(proj(COL_KA, DA_WIDTH), bd_ref, gk_ref).astype(BF16)

        g = proj(N_MAIN, LANES).T[:ML_GATES] + bif_ref[...]
        row = lax.broadcasted_iota(jnp.int32, g.shape, 0)
        logsig = jnp.minimum(g, 0.0) - jnp.log(1.0 + jnp.exp(-jnp.abs(g)))
        gt_ref[0, :, rows] = jnp.where((row // ML_HEADS) % 2 == 1, logsig, g)

        xm_ref[0, rows, :] = proj(COL_XM, ML_WIDTH).astype(BF16)

    for j in range(nslab):
        pieces = []
        for c in range(j * cps, (j + 1) * cps):
            pieces.append(normed(c))
            values(c, pieces[-1])
        projections(j, pieces[0] if cps == 1 else jnp.concatenate(pieces, axis=0))


def _in_proj(x, sc, sh, norm_w, w_t, bif, extra, *, latent, n_keys, key_off):
    b, t, d = x.shape
    tm = min(IN_TILE, t)
    assert key_off % tm == 0
    kblk = key_off // tm
    grid = (b, t // tm)
    const2 = lambda bi, i: (0, 0)
    mod_map = (lambda bi, i: (bi, 0, 0)) if latent else (lambda bi, i: (0, 0, 0))
    tok = lambda width, dtype: (pl.BlockSpec((1, tm, width), lambda bi, i: (bi, i, 0)),
                                jax.ShapeDtypeStruct((b, t, width), dtype))
    keys = (pl.BlockSpec((1, tm, DA_WIDTH), lambda bi, i: (bi, kblk + i, 0)),
            jax.ShapeDtypeStruct((b, n_keys, DA_WIDTH), BF16))
    cpt = tm // ATT_KC
    n_val = DA_WIDTH + ML_WIDTH
    vals = (pl.BlockSpec((1, cpt, n_val, ATT_KC), lambda bi, i: (bi, kblk + i, 0, 0)),
            jax.ShapeDtypeStruct((b, n_keys // ATT_KC, n_val, ATT_KC), BF16))
    in_specs = [pl.BlockSpec((1, tm, d), lambda bi, i: (bi, i, 0)),
                pl.BlockSpec((1, 1, d), mod_map),
                pl.BlockSpec((1, 1, d), mod_map),
                pl.BlockSpec((1, d), const2),
                pl.BlockSpec(w_t.shape, const2),
                pl.BlockSpec(bif.shape, const2)]
    if latent:
        bd, gq, gk, cos, sin = extra
        in_specs += [pl.BlockSpec(bd.shape, const2), pl.BlockSpec(gq.shape, const2),
                     pl.BlockSpec(gk.shape, const2),
                     pl.BlockSpec((tm, LANES), lambda bi, i: (i, 0)),
                     pl.BlockSpec((tm, LANES), lambda bi, i: (i, 0))]
        outs = [tok(DA_WIDTH, BF16), keys, vals, tok(DA_WIDTH, BF16)] + [tok(ML_WIDTH, BF16)] * 3
        aliases = {}
    else:
        bd, gk, _, _ = extra
        in_specs += [pl.BlockSpec(bd.shape, const2), pl.BlockSpec(gk.shape, const2),
                     pl.BlockSpec(memory_space=pl.ANY), pl.BlockSpec(memory_space=pl.ANY)]
        outs = [keys, vals, tok(ML_WIDTH, BF16)]
        aliases = {len(in_specs) - 2: 0, len(in_specs) - 1: 1}
    outs.append((pl.BlockSpec((1, ML_GATES, tm), lambda bi, i: (bi, 0, i)),
                 jax.ShapeDtypeStruct((b, ML_GATES, t), F32)))
    return pl.pallas_call(
        functools.partial(_inproj_kernel, latent=latent),
        grid=grid,
        in_specs=in_specs,
        out_specs=[o[0] for o in outs],
        out_shape=[o[1] for o in outs],
        input_output_aliases=aliases,
        compiler_params=pltpu.CompilerParams(dimension_semantics=("parallel", "parallel"),
                                             vmem_limit_bytes=VMEM_LIMIT),
        name="in_proj_latent" if latent else "in_proj_context",
    )(x, sc, sh, norm_w, w_t, bif, *extra)


def _attn_kernel(bound_ref, q_ref, k_ref, vt_ref, z_ref, lam_ref, hg_ref, o_ref, s_scr, e_scr, acc_scr, qm_scr):
    nblk = q_ref.shape[1] // ATT_BLK
    nkc = k_ref.shape[1] // ATT_KC
    sub = ATT_KC // 8
    lam = lam_ref[:, 0:1]
    nt = (((1,), (1,)), ((), ()))
    neg = jnp.full((8, ATT_BLK), -jnp.inf, F32)
    zero8 = jnp.zeros((8, ATT_BLK), F32)
    bound = bound_ref[0]

    def mask_queries(j):
        q = q_ref[0, j * ATT_BLK:(j + 1) * ATT_BLK, :]
        lane = lax.broadcasted_iota(jnp.int32, q.shape, 1)
        zero = jnp.zeros_like(q)
        qm_scr[0] = jnp.where(lane < DA_HD, q, zero)
        qm_scr[1] = jnp.where(lane >= DA_HD, q, zero)

    def scores(c, mp):
        kc = k_ref[0, pl.ds(pl.multiple_of(c * ATT_KC, ATT_KC), ATT_KC), :]
        return lax.dot_general(kc, qm_scr[mp], nt, preferred_element_type=F32).reshape(sub, 8, ATT_BLK)

    def exps(c, par, mp, st, stab8, lacc):
        e = jnp.exp2(st - stab8[None])
        e_scr[par, mp, c] = e.reshape(ATT_KC, ATT_BLK)
        return lacc + jnp.sum(e, axis=0)

    def values(c, par, r8):
        e1 = e_scr[par, 0, c].reshape(sub, 8, ATT_BLK)
        e2 = e_scr[par, 1, c].reshape(sub, 8, ATT_BLK)
        pt = (e1 - e2 * r8[None]).reshape(ATT_KC, ATT_BLK).astype(BF16)
        acc_scr[...] += jnp.dot(vt_ref[0, c], pt, preferred_element_type=F32)

    def finish(j, inv_l1):
        o = (acc_scr[...] * inv_l1).T
        ms = jnp.mean(o * o, axis=-1, keepdims=True)
        on = o * lax.rsqrt(ms + EPS) * hg_ref[...]
        rows = slice(j * ATT_BLK, (j + 1) * ATT_BLK)
        o_ref[0, rows, :] = (on * (1.0 - LAM_INIT) * z_ref[0, rows, :].astype(F32)).astype(BF16)

    def pipeline(exact_max):
        d_exp = 1 if exact_max else 0
        d_val = d_exp + 1
        bound8 = jnp.full((8, ATT_BLK), bound, F32)
        m8, r8, inv_l1 = {}, {}, {}
        for j in range(nblk + d_val):
            do_s, do_e, do_v = j < nblk, d_exp <= j < nblk + d_exp, d_val <= j
            if do_s:
                mask_queries(j)
            if do_v:
                acc_scr[...] = jnp.zeros_like(acc_scr)

            def body(c, carry, j=j, do_s=do_s, do_e=do_e, do_v=do_v):
                macc, lacc = carry
                if exact_max:
                    if do_s:
                        macc = list(macc)
                        for mp in range(2):
                            st = scores(c, mp)
                            s_scr[j % 2, mp, c] = st.reshape(ATT_KC, ATT_BLK)
                            macc[mp] = jnp.maximum(macc[mp], jnp.max(st, axis=0))
                        macc = tuple(macc)
                    if do_e:
                        jb = j - d_exp
                        lacc = tuple(
                            exps(c, jb % 2, mp, s_scr[jb % 2, mp, c].reshape(sub, 8, ATT_BLK), m8[jb][mp], lacc[mp])
                            for mp in range(2))
                elif do_s:
                    lacc = tuple(exps(c, j % 2, mp, scores(c, mp), bound8, lacc[mp]) for mp in range(2))
                if do_v:
                    values(c, (j - d_val) % 2, r8[j - d_val])
                return macc, lacc

            macc, lacc = lax.fori_loop(0, nkc, body, ((neg, neg), (zero8, zero8)),
                                       unroll=ATT_UNROLL if exact_max else nkc)
            if do_v:
                finish(j - d_val, inv_l1.pop(j - d_val))
                r8.pop(j - d_val)
            if do_e:
                l1 = jnp.sum(lacc[0], axis=0, keepdims=True)
                l2 = jnp.sum(lacc[1], axis=0, keepdims=True)
                r8[j - d_exp] = jnp.broadcast_to(lam * l1 / l2, (8, ATT_BLK))
                inv_l1[j - d_exp] = 1.0 / l1
                m8.pop(j - d_exp, None)
            if exact_max and do_s:
                m8[j] = tuple(jnp.broadcast_to(jnp.max(a, axis=0, keepdims=True), (8, ATT_BLK)) for a in macc)

    use_bound = bound <= 0.5 * (126.0 - ATT_EXP2_HEADROOM)

    @pl.when(use_bound)
    def _():
        pipeline(False)

    @pl.when(jnp.logical_not(use_bound))
    def _():
        pipeline(True)


def _diff_attn(bound, qa, k_all, vt_all, za, lam, head_g):
    b, t, _ = qa.shape
    n_keys = k_all.shape[1]
    nkc = n_keys // ATT_KC
    assert t % ATT_BLK == 0 and n_keys % ATT_KC == 0 and nkc % ATT_UNROLL == 0
    qmap = lambda bi, h: (bi, 0, h)
    const2 = lambda bi, h: (0, 0)
    return pl.pallas_call(
        _attn_kernel,
        grid=(b, DA_HEADS),
        in_specs=[pl.BlockSpec(memory_space=pltpu.SMEM),
                  pl.BlockSpec((1, t, DA_HEAD_W), qmap),
                  pl.BlockSpec((1, n_keys, DA_HEAD_W), qmap),
                  pl.BlockSpec((1, nkc, DA_HEAD_W, ATT_KC), lambda bi, h: (bi, 0, h, 0)),
                  pl.BlockSpec((1, t, DA_HEAD_W), qmap),
                  pl.BlockSpec((1, LANES), const2),
                  pl.BlockSpec((1, DA_HEAD_W), const2)],
        out_specs=pl.BlockSpec((1, t, DA_HEAD_W), qmap),
        out_shape=jax.ShapeDtypeStruct((b, t, DA_WIDTH), BF16),
        scratch_shapes=[pltpu.VMEM((2, 2, nkc, ATT_KC, ATT_BLK), F32),
                        pltpu.VMEM((2, 2, nkc, ATT_KC, ATT_BLK), F32),
                        pltpu.VMEM((DA_HEAD_W, ATT_BLK), F32),
                        pltpu.VMEM((2, ATT_BLK, DA_HEAD_W), BF16)],
        compiler_params=pltpu.CompilerParams(dimension_semantics=("parallel", "arbitrary"),
                                             vmem_limit_bytes=VMEM_LIMIT),
        name="diff_attn",
    )(bound, qa, k_all, vt_all, za, lam, head_g)


def _conv_silu(x, w, b, pad_s):
    t = x.shape[0]
    edge = jnp.zeros((8, x.shape[1]), F32)
    pad_s[0:8, :] = edge
    pad_s[8:8 + t, :] = x.astype(F32)
    pad_s[8 + t:16 + t, :] = edge
    y = b + w[0:1, :] * pad_s[7:7 + t, :]
    y = y + w[1:2, :] * pad_s[8:8 + t, :]
    y = y + w[2:3, :] * pad_s[9:9 + t, :]
    return _silu(y)


def _rows_to_cols(rows, eye):
    nt = (((1,), (1,)), ((), ()))
    out = None
    for part in _split3(rows):
        term = lax.dot_general(eye, part, nt, preferred_element_type=F32)
        out = term if out is None else out + term
    return out


def _cumsum_rows(rows, tri):
    out = None
    for part in _split3(rows):
        term = jnp.dot(part, tri, preferred_element_type=F32)
        out = term if out is None else out + term
    return out


def _cummax_rows(rows, reverse):
    n = rows.shape[1]
    lane = lax.broadcasted_iota(jnp.int32, rows.shape, 1)
    sh = 1
    while sh < n:
        if reverse:
            shifted = jnp.where(lane < n - sh, pltpu.roll(rows, n - sh, 1), -jnp.inf)
        else:
            shifted = jnp.where(lane >= sh, pltpu.roll(rows, sh, 1), -jnp.inf)
        rows = jnp.maximum(rows, shifted)
        sh *= 2
    return rows


def _mlstm_kernel(xl_ref, xc_ref, vt_ref, gl_ref, gc_ref, zb_ref, ob_ref,
                  cw_ref, cb_ref, wqt_ref, wk_ref, hg_ref, sk_ref, o_ref,
                  xcv_s, qt_s, k_s, vat_s, hf_s, hb_s, upd_s, stb_s, r1_s, bcol_s, pad_s, xb_s):
    hp = pl.program_id(1)
    n_lat = xl_ref.shape[1]
    nchunk = n_lat // ML_L
    L = ML_L
    nt = (((1,), (1,)), ((), ()))

    r_i = lax.broadcasted_iota(jnp.int32, (L, L), 0)
    c_i = lax.broadcasted_iota(jnp.int32, (L, L), 1)
    eye = jnp.where(r_i == c_i, 1.0, 0.0).astype(BF16)
    tris = (jnp.where(r_i <= c_i, 1.0, 0.0).astype(BF16), jnp.where(r_i >= c_i, 1.0, 0.0).astype(BF16))
    visible = (r_i <= c_i, r_i >= c_i)
    row = lax.broadcasted_iota(jnp.int32, (VA_ROWS - ML_HD, L), 0)
    ones_row = jnp.where(row == 0, 1.0, 0.0).astype(BF16)
    lane8 = lax.broadcasted_iota(jnp.int32, (8, L), 1)
    zero11 = jnp.zeros((1, 1), F32)

    gates = {}
    per_slot = {}

    def prepare(hh, x_ref, slot0):
        cols_h = slice(hh * ML_HD, (hh + 1) * ML_HD)
        n = x_ref.shape[1]
        xc = _conv_silu(x_ref[0, :, cols_h], cw_ref[:, cols_h], cb_ref[:, cols_h], pad_s)
        xb_s[0:n, :] = xc.astype(BF16)
        xb = xb_s[0:n, :]
        qt = lax.dot_general(wqt_ref[hh], xb, nt, preferred_element_type=F32).astype(BF16)
        for j in range(n // L):
            qt_s[hh, slot0 + j] = qt[:, j * L:(j + 1) * L]
        k = jnp.dot(xb, wk_ref[hh], preferred_element_type=F32) * (ML_HD ** -0.5)
        k_s[hh, slot0 * L:slot0 * L + n, :] = k.astype(BF16)
        return xc

    def gate_algebra(hh, d, g_ref, r0):
        head = hp * ML_HPP + hh
        ig = g_ref[0, (2 * d) * ML_HEADS + head] * LOG2E
        lf = g_ref[0, (2 * d + 1) * ML_HEADS + head] * LOG2E
        n = ig.shape[0]
        if n == 1:
            ig = jnp.broadcast_to(ig, (8, L))
            lf = jnp.broadcast_to(lf, (8, L))
        a = _cumsum_rows(lf, tris[d])
        bb = ig - a
        pick = (lane8 == L - 1) if d == 0 else (lane8 == 0)
        f_tot = jnp.sum(jnp.where(pick, a, 0.0), axis=-1, keepdims=True)
        b_max = jnp.max(bb, axis=-1, keepdims=True)
        cmax = _cummax_rows(bb, d == 1)
        cols = _rows_to_cols(bb, eye)
        for j in range(n):
            bcol_s[hh * 2 + d, r0 + j] = jnp.broadcast_to(cols[:, j:j + 1], (L, ML_HD))
            per_slot[hh, d, r0 + j] = (a[j:j + 1], cmax[j:j + 1], f_tot[j:j + 1], b_max[j:j + 1])

    for hh in range(ML_HPP):
        gate_algebra(hh, 0, gl_ref, 0)
        prepare(hh, xc_ref, nchunk)
        gate_algebra(hh, 0, gc_ref, nchunk)
        gate_algebra(hh, 1, gl_ref, 0)
        xcv_s[hh] = prepare(hh, xl_ref, 0)
        gate_algebra(hh, 1, gc_ref, nchunk)
        cols_h = slice(hh * ML_HD, (hh + 1) * ML_HD)
        for j in range(nchunk + 1):
            vat_s[hh, j, 0:ML_HD, :] = vt_ref[0, j, cols_h, :]
            vat_s[hh, j, ML_HD:VA_ROWS, :] = ones_row
        for d in range(2):
            m = zero11
            order = [nchunk] + (list(range(nchunk)) if d == 0 else list(range(nchunk - 1, -1, -1)))
            for slot in order:
                a_row, cmax_row, f_tot, b_max = per_slot[hh, d, slot]
                m_new = jnp.maximum(f_tot + m, f_tot + b_max)
                gates[hh, d, slot] = (a_row, cmax_row, f_tot, m, m_new)
                m = m_new

    latent = [(hh, d, c) for hh in range(ML_HPP) for d in range(2) for c in range(nchunk)]
    every = [(hh, d, nchunk) for hh in range(ML_HPP) for d in range(2)] + latent

    def scores(item):
        hh, _, c = item
        return jnp.dot(k_s[hh, c * L:(c + 1) * L, :], qt_s[hh, c], preferred_element_type=F32)

    def decayed(item, st):
        hh, d, c = item
        _, cmax_row, _, m_old, _ = gates[item]
        mm = jnp.maximum(cmax_row, m_old)
        bcol = bcol_s[hh * 2 + d, c]
        logd = jnp.where(visible[d], jnp.concatenate([bcol] * (L // ML_HD), axis=1) - mm, -jnp.inf)
        return (st * jnp.exp2(logd)).astype(BF16)

    def increment(item):
        hh, d, slot = item
        _, _, f_tot, _, m_new = gates[item]
        kc = k_s[hh, slot * L:(slot + 1) * L, :]
        kw = kc * jnp.exp2(bcol_s[hh * 2 + d, slot] + (f_tot - m_new)).astype(BF16)
        upd_s[hh * 2 + d, slot] = jnp.dot(vat_s[hh, slot], kw, preferred_element_type=F32)

    n_lat = len(latent)
    s_val, p_val = {}, {}
    for t in range(n_lat + 2):
        if t < n_lat:
            s_val[t] = scores(latent[t])
        if 0 <= t - 1 < n_lat:
            p_val[t - 1] = decayed(latent[t - 1], s_val.pop(t - 1))
        if 0 <= t - 2 < n_lat:
            hh, d, c = latent[t - 2]
            r1_s[hh * 2 + d, c] = jnp.dot(vat_s[hh, c], p_val.pop(t - 2), preferred_element_type=F32)
        if t < len(every):
            increment(every[t])
    for t in range(n_lat + 2, len(every)):
        increment(every[t])

    for hh in range(ML_HPP):
        for d in range(2):
            sd = hh * 2 + d
            st = upd_s[sd, nchunk]
            for c in (range(nchunk) if d == 0 else range(nchunk - 1, -1, -1)):
                _, _, f_tot, m_old, m_new = gates[hh, d, c]
                stb_s[sd, c] = st.astype(BF16)
                st = jnp.exp2(f_tot + m_old - m_new) * st + upd_s[sd, c]

    def carried(item):
        hh, d, c = item
        return jnp.dot(stb_s[hh * 2 + d, c], qt_s[hh, c], preferred_element_type=F32)

    def emit(item, r2):
        hh, d, c = item
        a_row, cmax_row, _, m_old, _ = gates[item]
        mm = jnp.maximum(cmax_row, m_old)
        tot = r1_s[hh * 2 + d, c] + jnp.exp2(m_old - mm) * r2
        den = tot[ML_HD:ML_HD + 1, :]
        floor = jnp.exp2(-(a_row + mm))
        (hf_s if d == 0 else hb_s)[hh, c] = tot[0:ML_HD, :] * (1.0 / jnp.maximum(jnp.abs(den), floor))

    r2_val = {}
    for t in range(n_lat + 1):
        if t < n_lat:
            r2_val[t] = carried(latent[t])
        if t >= 1:
            emit(latent[t - 1], r2_val.pop(t - 1))


    for hh in range(ML_HPP):
        cols_h = slice(hh * ML_HD, (hh + 1) * ML_HD)
        for c in range(nchunk):
            ht = hf_s[hh, c] + hb_s[hh, c]
            ms = jnp.mean(ht * ht, axis=0, keepdims=True)
            hn = (ht * lax.rsqrt(ms + EPS)).T * hg_ref[:, cols_h]
            rows = slice(c * L, (c + 1) * L)
            y = ((ob_ref[0, rows, cols_h].astype(F32) * hn + sk_ref[:, cols_h] * xcv_s[hh, rows, :])
                 * zb_ref[0, rows, cols_h].astype(F32))
            o_ref[0, rows, cols_h] = y.astype(BF16)


def _mlstm(xm_l, xm_c, vt_all, gt_l, gt_c, zb, ob, conv_w, conv_b, wqt, wk, head_g, skip):
    b, t, _ = xm_l.shape
    n_ctx = xm_c.shape[1]
    nchunk = t // ML_L
    nslot = nchunk + 1
    assert t % (8 * ML_L) == 0 and n_ctx == ML_L and vt_all.shape[1] == nslot
    gl = gt_l.reshape(b, ML_GATES, nchunk, ML_L)
    gc = gt_c.reshape(b, ML_GATES, 1, ML_L)
    pw = ML_HPP * ML_HD
    vt_blk0 = DA_WIDTH // pw
    tokmap = lambda bi, h: (bi, 0, h)
    gmap = lambda bi, h: (bi, 0, 0, 0)
    hvec = lambda bi, h: (0, h)
    wmap = lambda bi, h: (h, 0, 0)
    return pl.pallas_call(
        _mlstm_kernel,
        grid=(b, ML_HEADS // ML_HPP),
        in_specs=[pl.BlockSpec((1, t, pw), tokmap),
                  pl.BlockSpec((1, n_ctx, pw), tokmap),
                  pl.BlockSpec((1, nslot, pw, ML_L), lambda bi, h: (bi, 0, vt_blk0 + h, 0)),
                  pl.BlockSpec((1, ML_GATES, nchunk, ML_L), gmap),
                  pl.BlockSpec((1, ML_GATES, 1, ML_L), gmap),
                  pl.BlockSpec((1, t, pw), tokmap),
                  pl.BlockSpec((1, t, pw), tokmap),
                  pl.BlockSpec((conv_w.shape[0], pw), hvec),
                  pl.BlockSpec((1, pw), hvec),
                  pl.BlockSpec((ML_HPP, ML_HD, ML_HD), wmap),
                  pl.BlockSpec((ML_HPP, ML_HD, ML_HD), wmap),
                  pl.BlockSpec((1, pw), hvec),
                  pl.BlockSpec((1, pw), hvec)],
        out_specs=pl.BlockSpec((1, t, pw), tokmap),
        out_shape=jax.ShapeDtypeStruct((b, t, ML_WIDTH), BF16),
        scratch_shapes=[pltpu.VMEM((ML_HPP, t, ML_HD), F32),
                        pltpu.VMEM((ML_HPP, nslot, ML_HD, ML_L), BF16),
                        pltpu.VMEM((ML_HPP, nslot * ML_L, ML_HD), BF16),
                        pltpu.VMEM((ML_HPP, nslot, VA_ROWS, ML_L), BF16),
                        pltpu.VMEM((ML_HPP, nchunk, ML_HD, ML_L), F32),
                        pltpu.VMEM((ML_HPP, nchunk, ML_HD, ML_L), F32),
                        pltpu.VMEM((ML_HPP * 2, nslot, VA_ROWS, ML_HD), F32),
                        pltpu.VMEM((ML_HPP * 2, nchunk, VA_ROWS, ML_HD), BF16),
                        pltpu.VMEM((ML_HPP * 2, nchunk, VA_ROWS, ML_L), F32),
                        pltpu.VMEM((ML_HPP * 2, nslot, ML_L, ML_HD), F32),
                        pltpu.VMEM((t + 16, ML_HD), F32),
                        pltpu.VMEM((t, ML_HD), BF16)],
        compiler_params=pltpu.CompilerParams(dimension_semantics=("parallel", "arbitrary"),
                                             vmem_limit_bytes=VMEM_LIMIT),
        name="mlstm",
    )(xm_l, xm_c, vt_all, gl, gc, zb, ob, conv_w, conv_b, wqt, wk, head_g, skip)


def _outproj_kernel(x_ref, sc_ref, sh_ref, nw_ref, ya_ref, yb_ref, gt_ref, wg_ref, woa_ref, wob_ref, wo_ref,
                    o_ref):
    d = x_ref.shape[2]
    nt = (((1,), (1,)), ((), ()))
    a = jnp.dot(ya_ref[0], woa_ref[...], preferred_element_type=F32)
    xf = x_ref[0]
    ms = jnp.mean(xf * xf, axis=-1, keepdims=True)
    yn = xf * lax.rsqrt(ms + EPS) * nw_ref[...]
    h = (yn * (1.0 + sc_ref[0]) + sh_ref[0]).astype(BF16)
    y = _sigmoid(lax.dot_general(h, wg_ref[0:d, :], nt, preferred_element_type=F32)) * a
    b = jnp.dot(yb_ref[0], wob_ref[...], preferred_element_type=F32)
    y = y + _sigmoid(lax.dot_general(h, wg_ref[d:2 * d, :], nt, preferred_element_type=F32)) * b
    o = jnp.dot(y.astype(BF16), wo_ref[...], preferred_element_type=F32)
    o_ref[0] = xf + gt_ref[0] * o


def _out_proj(x, sc, sh, norm_w, ya, yb, gt, wg, woa, wob, wo):
    b, t, d = x.shape
    tm = min(OUT_TILE, t)
    tok = lambda width: pl.BlockSpec((1, tm, width), lambda bi, i: (bi, i, 0))
    const2 = lambda bi, i: (0, 0)
    per_batch = pl.BlockSpec((1, 1, d), lambda bi, i: (bi, 0, 0))
    return pl.pallas_call(
        _outproj_kernel,
        grid=(b, t // tm),
        in_specs=[tok(d), per_batch, per_batch, pl.BlockSpec((1, d), const2),
                  tok(DA_WIDTH), tok(ML_WIDTH), per_batch,
                  pl.BlockSpec(wg.shape, const2),
                  pl.BlockSpec(woa.shape, const2), pl.BlockSpec(wob.shape, const2),
                  pl.BlockSpec(wo.shape, const2)],
        out_specs=tok(d),
        out_shape=jax.ShapeDtypeStruct((b, t, d), x.dtype),
        compiler_params=pltpu.CompilerParams(dimension_semantics=("parallel", "parallel"),
                                             vmem_limit_bytes=VMEM_LIMIT),
        name="out_proj",
    )(x, sc, sh, norm_w, ya, yb, gt, wg, woa, wob, wo)


def _rope_tables(n_tokens):
    rows = n_tokens // GRID_W
    row_id = jnp.repeat(jnp.arange(rows, dtype=F32), GRID_W)
    col_id = jnp.tile(jnp.arange(GRID_W, dtype=F32), rows)
    n_freq = DA_HD // 4
    inv_freq = ROPE_THETA ** (-jnp.arange(n_freq, dtype=F32) / n_freq)
    ang_r = row_id[:, None] * inv_freq
    ang_c = col_id[:, None] * inv_freq
    cos = jnp.concatenate([jnp.cos(ang_r)] * 2 + [jnp.cos(ang_c)] * 2, axis=-1)
    sin = jnp.concatenate([-jnp.sin(ang_r), jnp.sin(ang_r), -jnp.sin(ang_c), jnp.sin(ang_c)], axis=-1)
    return jnp.tile(cos, (1, LANES // DA_HD)), jnp.tile(sin, (1, LANES // DA_HD))


def kernel(x, c, ctx, c_ctx, norm_w, w_mod, b_mod, w_in, b_if, da_q_norm, da_k_norm, da_lambda_q1, da_lambda_k1, da_lambda_q2, da_lambda_k2, da_head_norm, w_out_a, ml_conv_w, ml_conv_b, ml_wq, ml_wk, ml_head_norm, ml_skip, w_out_b, w_o):
    assert w_mod.shape[0] == 1, "single-layer block"
    b, t, d = x.shape

    rows = -(-(b + 1) // 8) * 8
    cc = jnp.zeros((rows, d), F32).at[:b].set(c).at[b].set(c_ctx)
    mod, lam, bound = _modulation(cc, w_mod[0], b_mod, da_lambda_q1, da_lambda_k1, da_lambda_q2, da_lambda_k2,
                                  da_q_norm, da_k_norm)
    sh_l, sc_l, gt_l = (mod[:b, i * d:(i + 1) * d].reshape(b, 1, d) for i in range(3))
    sh_c, sc_c = (mod[b:b + 1, i * d:(i + 1) * d].reshape(1, 1, d) for i in range(2))

    w_t = jnp.swapaxes(w_in[0], 0, 1).astype(BF16)
    bif = b_if.reshape(ML_GATES, 1)

    grp = np.arange(2 * LANES) // DA_HD
    bd = jnp.asarray((grp[:, None] == grp[None, :]).astype(np.float32) / DA_HD, dtype=BF16)
    gq = jnp.tile(da_q_norm, (1, DA_WIDTH // DA_HD))
    gk = jnp.tile(da_k_norm, (1, DA_WIDTH // DA_HD))
    cos, sin = _rope_tables(t)

    n_keys = t + ctx.shape[1]
    qa, k_all, vt_all, za, xm_l, zb, ob, gtl = _in_proj(
        x, sc_l, sh_l, norm_w, w_t, bif, (bd, gq, gk, cos, sin), latent=True, n_keys=n_keys, key_off=0)
    k_all, vt_all, xm_c, gtc = _in_proj(
        ctx, sc_c, sh_c, norm_w, w_t, bif, (bd, gk, k_all, vt_all), latent=False, n_keys=n_keys, key_off=t)

    ya = _diff_attn(bound[0, :1], qa, k_all, vt_all, za, lam, da_head_norm)
    yb = _mlstm(xm_l, xm_c, vt_all, gtl, gtc, zb, ob, ml_conv_w[0], ml_conv_b,
                jnp.swapaxes(ml_wq[0], 1, 2).astype(BF16), ml_wk[0].astype(BF16), ml_head_norm, ml_skip)
    return _out_proj(x, sc_l, sh_l, norm_w, ya, yb, gt_l, w_t[COL_GG:], w_out_a[0].astype(BF16), w_out_b[0].astype(BF16),
                     w_o[0].astype(BF16))
```

```python
import functools

import jax
import jax.numpy as jnp
import numpy as np
from jax import lax
from jax.experimental import pallas as pl
from jax.experimental.pallas import tpu as pltpu

F32 = jnp.float32
BF16 = jnp.bfloat16

EPS = 1e-6
ROPE_THETA = 10000.0
GRID_W = 64
LAM_INIT = 0.8 - 0.6 * 1.0

DA_HEADS = 4
DA_HD = 64
DA_HEAD_W = 2 * DA_HD
DA_WIDTH = DA_HEADS * DA_HEAD_W

ML_HEADS = 4
ML_HD = 128
ML_WIDTH = ML_HEADS * ML_HD
ML_GATES = 4 * ML_HEADS

COL_QA, COL_KA, COL_VA, COL_ZA = (i * DA_WIDTH for i in range(4))
COL_XM, COL_VM, COL_ZB, COL_OB = (4 * DA_WIDTH + i * ML_WIDTH for i in range(4))
N_MAIN = 4 * DA_WIDTH + 4 * ML_WIDTH
COL_GG = N_MAIN + ML_GATES
ML_L = 256
VA_ROWS = ML_HD + 16
LOG2E = 1.4426950408889634
ML_HPP = 2

LANES = 128
VMEM_LIMIT = 56 * 1024 * 1024

IN_TILE = 512
IN_SLAB = 512
OUT_TILE = 1024
ATT_BLK = 256
ATT_KC = 256
ATT_UNROLL = 3
ATT_HPP = 2
ATT_EXP2_HEADROOM = 30.0
SCORE_BOUND_MARGIN = 1.02
Q_SCALE = DA_HD ** -0.5 * LOG2E


def _sigmoid(x):
    return 0.5 * jnp.tanh(0.5 * x) + 0.5


def _silu(x):
    h = 0.5 * x
    return h * jnp.tanh(h) + h


def _split3(x):
    hi = x.astype(BF16)
    r1 = x - hi.astype(F32)
    mid = r1.astype(BF16)
    lo = (r1 - mid.astype(F32)).astype(BF16)
    return hi, mid, lo


def _mod_kernel(cc_ref, w_ref, b_ref, lq1_ref, lk1_ref, lq2_ref, lk2_ref, gq_ref, gk_ref,
                mod_ref, lam_ref, bound_ref):
    a = _silu(cc_ref[...]).astype(BF16)
    mod_ref[...] = jnp.dot(a, w_ref[...].astype(BF16), preferred_element_type=F32) + b_ref[...]
    s1 = jnp.sum(lq1_ref[...] * lk1_ref[...], axis=-1, keepdims=True)
    s2 = jnp.sum(lq2_ref[...] * lk2_ref[...], axis=-1, keepdims=True)
    lam = jnp.exp(s1) - jnp.exp(s2) + LAM_INIT
    lam_ref[...] = jnp.broadcast_to(lam, lam_ref.shape)
    bq = jnp.max(jnp.abs(gq_ref[...]), axis=-1, keepdims=True)
    bk = jnp.max(jnp.abs(gk_ref[...]), axis=-1, keepdims=True)
    bound_ref[...] = jnp.broadcast_to(SCORE_BOUND_MARGIN * Q_SCALE * DA_HD * bq * bk, bound_ref.shape)


def _modulation(cc, w_mod, b_mod, lq1, lk1, lq2, lk2, gq, gk):
    rows, d = cc.shape
    n3 = w_mod.shape[1]
    vec = pl.BlockSpec((1, DA_HD), lambda j: (0, 0))
    return pl.pallas_call(
        _mod_kernel,
        grid=(n3 // d,),
        in_specs=[pl.BlockSpec((rows, d), lambda j: (0, 0)),
                  pl.BlockSpec((d, d), lambda j: (0, j)),
                  pl.BlockSpec((1, d), lambda j: (0, j)),
                  vec, vec, vec, vec, vec, vec],
        out_specs=[pl.BlockSpec((rows, d), lambda j: (0, j)),
                   pl.BlockSpec((1, LANES), lambda j: (0, 0)),
                   pl.BlockSpec((1, LANES), lambda j: (0, 0))],
        out_shape=[jax.ShapeDtypeStruct((rows, n3), F32),
                   jax.ShapeDtypeStruct((1, LANES), F32),
                   jax.ShapeDtypeStruct((1, LANES), F32)],
        compiler_params=pltpu.CompilerParams(dimension_semantics=("arbitrary",),
                                             vmem_limit_bytes=VMEM_LIMIT),
        name="modulation",
    )(cc, w_mod, b_mod, lq1, lk1, lq2, lk2, gq, gk)


def _group_rms(acc, bd_ref, g_ref):
    sq = (acc * acc).astype(BF16)
    n = bd_ref.shape[0]
    ms = jnp.concatenate([jnp.dot(sq[:, c:c + n], bd_ref[...], preferred_element_type=F32)
                          for c in range(0, acc.shape[1], n)], axis=1)
    return acc * lax.rsqrt(ms + EPS) * g_ref[...]


def _rope(x, cos, sin_signed, first_half):
    outs = []
    for c in range(x.shape[1] // LANES):
        xs = x[:, c * LANES:(c + 1) * LANES]
        partner = jnp.where(first_half, pltpu.roll(xs, LANES - 16, 1), pltpu.roll(xs, 16, 1))
        outs.append(xs * cos + partner * sin_signed)
    return jnp.concatenate(outs, axis=1)


def _inproj_kernel(x_ref, sc_ref, sh_ref, nw_ref, w_ref, bif_ref, *rest, latent):
    if latent:
        (bd_ref, gq_ref, gk_ref, cos_ref, sin_ref,
         qa_ref, ka_ref, vt_ref, za_ref, xm_ref, zb_ref, ob_ref, gt_ref) = rest
    else:
        bd_ref, gk_ref, _, _, ka_ref, vt_ref, xm_ref, gt_ref = rest

    nt = (((1,), (1,)), ((), ()))
    slab = min(IN_SLAB, x_ref.shape[1])
    nslab = x_ref.shape[1] // slab
    cps = slab // ATT_KC

    def normed(c):
        xf = x_ref[0, c * ATT_KC:(c + 1) * ATT_KC, :]
        ms = jnp.mean(xf * xf, axis=-1, keepdims=True)
        y = xf * lax.rsqrt(ms + EPS) * nw_ref[...]
        return (y * (1.0 + sc_ref[0]) + sh_ref[0]).astype(BF16)

    def values(c, hc):
        for r0, c0 in ((0, COL_VA), (DA_WIDTH, COL_VM)):
            vt_ref[0, c, r0:r0 + DA_WIDTH, :] = lax.dot_general(
                w_ref[c0:c0 + DA_WIDTH, :], hc, nt, preferred_element_type=F32).astype(BF16)

    def projections(j, h):
        rows = slice(j * slab, (j + 1) * slab)

        def proj(c0, width):
            return lax.dot_general(h, w_ref[c0:c0 + width, :], nt, preferred_element_type=F32)

        if latent:
            lane = lax.broadcasted_iota(jnp.int32, (slab, LANES), 1)
            first_half = (lane % 32) < 16
            cos = cos_ref[rows, :]
            sin = sin_ref[rows, :]
            q = _rope(_group_rms(proj(COL_QA, DA_WIDTH), bd_ref, gq_ref), cos, sin, first_half)
            qa_ref[0, rows, :] = (q * Q_SCALE).astype(BF16)
            k = _rope(_group_rms(proj(COL_KA, DA_WIDTH), bd_ref, gk_ref), cos, sin, first_half)
            ka_ref[0, rows, :] = k.astype(BF16)
            za_ref[0, rows, :] = _silu(proj(COL_ZA, DA_WIDTH)).astype(BF16)
            zb_ref[0, rows, :] = _silu(proj(COL_ZB, ML_WIDTH)).astype(BF16)
            ob_ref[0, rows, :] = _sigmoid(proj(COL_OB, ML_WIDTH)).astype(BF16)
        else:
            ka_ref[0, rows, :] = _group_rms(proj(COL_KA, DA_WIDTH), bd_ref, gk_ref).astype(BF16)

        g = proj(N_MAIN, LANES).T[:ML_GATES] + bif_ref[...]
        row = lax.broadcasted_iota(jnp.int32, g.shape, 0)
        logsig = jnp.minimum(g, 0.0) - jnp.log(1.0 + jnp.exp(-jnp.abs(g)))
        gt_ref[0, :, rows] = jnp.where((row // ML_HEADS) % 2 == 1, logsig, g)

        xm_ref[0, rows, :] = proj(COL_XM, ML_WIDTH).astype(BF16)

    for j in range(nslab):
        pieces = []
        for c in range(j * cps, (j + 1) * cps):
            pieces.append(normed(c))
            values(c, pieces[-1])
        projections(j, pieces[0] if cps == 1 else jnp.concatenate(pieces, axis=0))


def _in_proj(x, sc, sh, norm_w, w_t, bif, extra, *, latent, n_keys, key_off):
    b, t, d = x.shape
    tm = min(IN_TILE, t)
    assert key_off % tm == 0
    kblk = key_off // tm
    grid = (b, t // tm)
    const2 = lambda bi, i: (0, 0)
    mod_map = (lambda bi, i: (bi, 0, 0)) if latent else (lambda bi, i: (0, 0, 0))
    tok = lambda width, dtype: (pl.BlockSpec((1, tm, width), lambda bi, i: (bi, i, 0)),
                                jax.ShapeDtypeStruct((b, t, width), dtype))
    keys = (pl.BlockSpec((1, tm, DA_WIDTH), lambda bi, i: (bi, kblk + i, 0)),
            jax.ShapeDtypeStruct((b, n_keys, DA_WIDTH), BF16))
    cpt = tm // ATT_KC
    n_val = DA_WIDTH + ML_WIDTH
    vals = (pl.BlockSpec((1, cpt, n_val, ATT_KC), lambda bi, i: (bi, kblk + i, 0, 0)),
            jax.ShapeDtypeStruct((b, n_keys // ATT_KC, n_val, ATT_KC), BF16))
    in_specs = [pl.BlockSpec((1, tm, d), lambda bi, i: (bi, i, 0)),
                pl.BlockSpec((1, 1, d), mod_map),
                pl.BlockSpec((1, 1, d), mod_map),
                pl.BlockSpec((1, d), const2),
                pl.BlockSpec(w_t.shape, const2),
                pl.BlockSpec(bif.shape, const2)]
    if latent:
        bd, gq, gk, cos, sin = extra
        in_specs += [pl.BlockSpec(bd.shape, const2), pl.BlockSpec(gq.shape, const2),
                     pl.BlockSpec(gk.shape, const2),
                     pl.BlockSpec((tm, LANES), lambda bi, i: (i, 0)),
                     pl.BlockSpec((tm, LANES), lambda bi, i: (i, 0))]
        outs = [tok(DA_WIDTH, BF16), keys, vals, tok(DA_WIDTH, BF16)] + [tok(ML_WIDTH, BF16)] * 3
        aliases = {}
    else:
        bd, gk, _, _ = extra
        in_specs += [pl.BlockSpec(bd.shape, const2), pl.BlockSpec(gk.shape, const2),
                     pl.BlockSpec(memory_space=pl.ANY), pl.BlockSpec(memory_space=pl.ANY)]
        outs = [keys, vals, tok(ML_WIDTH, BF16)]
        aliases = {len(in_specs) - 2: 0, len(in_specs) - 1: 1}
    outs.append((pl.BlockSpec((1, ML_GATES, tm), lambda bi, i: (bi, 0, i)),
                 jax.ShapeDtypeStruct((b, ML_GATES, t), F32)))
    return pl.pallas_call(
        functools.partial(_inproj_kernel, latent=latent),
        grid=grid,
        in_specs=in_specs,
        out_specs=[o[0] for o in outs],
        out_shape=[o[1] for o in outs],
        input_output_aliases=aliases,
        compiler_params=pltpu.CompilerParams(dimension_semantics=("parallel", "parallel"),
                                             vmem_limit_bytes=VMEM_LIMIT),
        name="in_proj_latent" if latent else "in_proj_context",
    )(x, sc, sh, norm_w, w_t, bif, *extra)


def _attn_kernel(bound_ref, q_ref, k_ref, vt_ref, z_ref, lam_ref, hg_ref, o_ref, s_scr, e_scr, acc_scr, qm_scr):
    nblk = q_ref.shape[1] // ATT_BLK
    nkc = k_ref.shape[1] // ATT_KC
    sub = ATT_KC // 8
    items = [(hh, j) for hh in range(ATT_HPP) for j in range(nblk)]

    def head_cols(hh):
        return slice(hh * DA_HEAD_W, (hh + 1) * DA_HEAD_W)
    lam = lam_ref[:, 0:1]
    nt = (((1,), (1,)), ((), ()))
    neg = jnp.full((8, ATT_BLK), -jnp.inf, F32)
    zero8 = jnp.zeros((8, ATT_BLK), F32)
    bound = bound_ref[0]

    def mask_queries(item):
        hh, j = item
        q = q_ref[0, j * ATT_BLK:(j + 1) * ATT_BLK, head_cols(hh)]
        lane = lax.broadcasted_iota(jnp.int32, q.shape, 1)
        zero = jnp.zeros_like(q)
        qm_scr[0] = jnp.where(lane < DA_HD, q, zero)
        qm_scr[1] = jnp.where(lane >= DA_HD, q, zero)

    def scores(item, c, mp):
        kc = k_ref[0, pl.ds(pl.multiple_of(c * ATT_KC, ATT_KC), ATT_KC), head_cols(item[0])]
        return lax.dot_general(kc, qm_scr[mp], nt, preferred_element_type=F32).reshape(sub, 8, ATT_BLK)

    def exps(c, par, mp, st, stab8, lacc):
        e = jnp.exp2(st - stab8[None])
        e_scr[par, mp, c] = e.reshape(ATT_KC, ATT_BLK)
        return lacc + jnp.sum(e, axis=0)

    def values(item, c, par, r8):
        e1 = e_scr[par, 0, c].reshape(sub, 8, ATT_BLK)
        e2 = e_scr[par, 1, c].reshape(sub, 8, ATT_BLK)
        pt = (e1 - e2 * r8[None]).reshape(ATT_KC, ATT_BLK).astype(BF16)
        acc_scr[...] += jnp.dot(vt_ref[0, c, head_cols(item[0]), :], pt, preferred_element_type=F32)

    def finish(item, inv_l1):
        hh, j = item
        o = (acc_scr[...] * inv_l1).T
        ms = jnp.mean(o * o, axis=-1, keepdims=True)
        on = o * lax.rsqrt(ms + EPS) * hg_ref[...]
        rows = slice(j * ATT_BLK, (j + 1) * ATT_BLK)
        gate = z_ref[0, rows, head_cols(hh)].astype(F32)
        o_ref[0, rows, head_cols(hh)] = (on * (1.0 - LAM_INIT) * gate).astype(BF16)

    def pipeline(exact_max):
        d_exp = 1 if exact_max else 0
        d_val = d_exp + 1
        bound8 = jnp.full((8, ATT_BLK), bound, F32)
        m8, r8, inv_l1 = {}, {}, {}
        nit = len(items)
        for j in range(nit + d_val):
            do_s, do_e, do_v = j < nit, d_exp <= j < nit + d_exp, d_val <= j
            if do_s:
                mask_queries(items[j])
            if do_v:
                acc_scr[...] = jnp.zeros_like(acc_scr)

            def body(c, carry, j=j, do_s=do_s, do_e=do_e, do_v=do_v):
                macc, lacc = carry
                if exact_max:
                    if do_s:
                        macc = list(macc)
                        for mp in range(2):
                            st = scores(items[j], c, mp)
                            s_scr[j % 2, mp, c] = st.reshape(ATT_KC, ATT_BLK)
                            macc[mp] = jnp.maximum(macc[mp], jnp.max(st, axis=0))
                        macc = tuple(macc)
                    if do_e:
                        jb = j - d_exp
                        lacc = tuple(
                            exps(c, jb % 2, mp, s_scr[jb % 2, mp, c].reshape(sub, 8, ATT_BLK), m8[jb][mp], lacc[mp])
                            for mp in range(2))
                elif do_s:
                    lacc = tuple(exps(c, j % 2, mp, scores(items[j], c, mp), bound8, lacc[mp]) for mp in range(2))
                if do_v:
                    values(items[j - d_val], c, (j - d_val) % 2, r8[j - d_val])
                return macc, lacc

            macc, lacc = lax.fori_loop(0, nkc, body, ((neg, neg), (zero8, zero8)),
                                       unroll=ATT_UNROLL if exact_max else nkc)
            if do_v:
                finish(items[j - d_val], inv_l1.pop(j - d_val))
                r8.pop(j - d_val)
            if do_e:
                l1 = jnp.sum(lacc[0], axis=0, keepdims=True)
                l2 = jnp.sum(lacc[1], axis=0, keepdims=True)
                r8[j - d_exp] = jnp.broadcast_to(lam * l1 / l2, (8, ATT_BLK))
                inv_l1[j - d_exp] = 1.0 / l1
                m8.pop(j - d_exp, None)
            if exact_max and do_s:
                m8[j] = tuple(jnp.broadcast_to(jnp.max(a, axis=0, keepdims=True), (8, ATT_BLK)) for a in macc)

    use_bound = bound <= 0.5 * (126.0 - ATT_EXP2_HEADROOM)

    @pl.when(use_bound)
    def _():
        pipeline(False)

    @pl.when(jnp.logical_not(use_bound))
    def _():
        pipeline(True)


def _diff_attn(bound, qa, k_all, vt_all, za, lam, head_g):
    b, t, _ = qa.shape
    n_keys = k_all.shape[1]
    nkc = n_keys // ATT_KC
    assert t % ATT_BLK == 0 and n_keys % ATT_KC == 0 and nkc % ATT_UNROLL == 0
    pw = ATT_HPP * DA_HEAD_W
    qmap = lambda bi, h: (bi, 0, h)
    const2 = lambda bi, h: (0, 0)
    return pl.pallas_call(
        _attn_kernel,
        grid=(b, DA_HEADS // ATT_HPP),
        in_specs=[pl.BlockSpec(memory_space=pltpu.SMEM),
                  pl.BlockSpec((1, t, pw), qmap),
                  pl.BlockSpec((1, n_keys, pw), qmap),
                  pl.BlockSpec((1, nkc, pw, ATT_KC), lambda bi, h: (bi, 0, h, 0)),
                  pl.BlockSpec((1, t, pw), qmap),
                  pl.BlockSpec((1, LANES), const2),
                  pl.BlockSpec((1, DA_HEAD_W), const2)],
        out_specs=pl.BlockSpec((1, t, pw), qmap),
        out_shape=jax.ShapeDtypeStruct((b, t, DA_WIDTH), BF16),
        scratch_shapes=[pltpu.VMEM((2, 2, nkc, ATT_KC, ATT_BLK), F32),
                        pltpu.VMEM((2, 2, nkc, ATT_KC, ATT_BLK), F32),
                        pltpu.VMEM((DA_HEAD_W, ATT_BLK), F32),
                        pltpu.VMEM((2, ATT_BLK, DA_HEAD_W), BF16)],
        compiler_params=pltpu.CompilerParams(dimension_semantics=("parallel", "arbitrary"),
                                             vmem_limit_bytes=VMEM_LIMIT),
        name="diff_attn",
    )(bound, qa, k_all, vt_all, za, lam, head_g)


def _rows_to_cols(rows, eye):
    nt = (((1,), (1,)), ((), ()))
    out = None
    for part in _split3(rows):
        term = lax.dot_general(eye, part, nt, preferred_element_type=F32)
        out = term if out is None else out + term
    return out


def _cumsum_rows(rows, tri):
    out = None
    for part in _split3(rows):
        term = jnp.dot(part, tri, preferred_element_type=F32)
        out = term if out is None else out + term
    return out


def _cummax_rows(rows, reverse):
    n = rows.shape[1]
    lane = lax.broadcasted_iota(jnp.int32, rows.shape, 1)
    sh = 1
    while sh < n:
        if reverse:
            shifted = jnp.where(lane < n - sh, pltpu.roll(rows, n - sh, 1), -jnp.inf)
        else:
            shifted = jnp.where(lane >= sh, pltpu.roll(rows, sh, 1), -jnp.inf)
        rows = jnp.maximum(rows, shifted)
        sh *= 2
    return rows


def _mlstm_kernel(xl_ref, xc_ref, vt_ref, gl_ref, gc_ref, zb_ref, ob_ref,
                  cw_ref, cb_ref, wqt_ref, wk_ref, hg_ref, sk_ref, o_ref,
                  xcv_s, qt_s, k_s, vat_s, hf_s, hb_s, upd_s, stb_s, r1_s, bcol_s, pad_s):
    hp = pl.program_id(1)
    n_lat = xl_ref.shape[1]
    nchunk = n_lat // ML_L
    L = ML_L
    nt = (((1,), (1,)), ((), ()))

    r_i = lax.broadcasted_iota(jnp.int32, (L, L), 0)
    c_i = lax.broadcasted_iota(jnp.int32, (L, L), 1)
    eye = jnp.where(r_i == c_i, 1.0, 0.0).astype(BF16)
    tris = (jnp.where(r_i <= c_i, 1.0, 0.0).astype(BF16), jnp.where(r_i >= c_i, 1.0, 0.0).astype(BF16))
    visible = (r_i <= c_i, r_i >= c_i)
    row = lax.broadcasted_iota(jnp.int32, (VA_ROWS - ML_HD, L), 0)
    ones_row = jnp.where(row == 0, 1.0, 0.0).astype(BF16)
    lane8 = lax.broadcasted_iota(jnp.int32, (8, L), 1)
    zero11 = jnp.zeros((1, 1), F32)

    gates = {}
    per_slot = {}

    def park(hh, x_ref):
        n = x_ref.shape[1]
        edge = jnp.zeros((8, ML_HD), F32)
        pad_s[0:8, :] = edge
        pad_s[8:8 + n, :] = x_ref[0, :, hh * ML_HD:(hh + 1) * ML_HD].astype(F32)
        pad_s[8 + n:16 + n, :] = edge

    def conv_chunk(hh, c, slot, keep):
        cols_h = slice(hh * ML_HD, (hh + 1) * ML_HD)
        w = cw_ref[:, cols_h]
        y = cb_ref[:, cols_h] + w[0:1, :] * pad_s[7 + c * L:7 + (c + 1) * L, :]
        y = y + w[1:2, :] * pad_s[8 + c * L:8 + (c + 1) * L, :]
        y = y + w[2:3, :] * pad_s[9 + c * L:9 + (c + 1) * L, :]
        xc = _silu(y)
        if keep:
            xcv_s[hh, c * L:(c + 1) * L, :] = xc
        xb = xc.astype(BF16)
        qt_s[hh, slot] = lax.dot_general(wqt_ref[hh], xb, nt, preferred_element_type=F32).astype(BF16)
        k = jnp.dot(xb, wk_ref[hh], preferred_element_type=F32) * (ML_HD ** -0.5)
        k_s[hh, slot * L:(slot + 1) * L, :] = k.astype(BF16)

    def gate_algebra(hh, d, g_ref, r0):
        head = hp * ML_HPP + hh
        ig = g_ref[0, (2 * d) * ML_HEADS + head] * LOG2E
        lf = g_ref[0, (2 * d + 1) * ML_HEADS + head] * LOG2E
        n = ig.shape[0]
        if n == 1:
            ig = jnp.broadcast_to(ig, (8, L))
            lf = jnp.broadcast_to(lf, (8, L))
        a = _cumsum_rows(lf, tris[d])
        bb = ig - a
        pick = (lane8 == L - 1) if d == 0 else (lane8 == 0)
        f_tot = jnp.sum(jnp.where(pick, a, 0.0), axis=-1, keepdims=True)
        b_max = jnp.max(bb, axis=-1, keepdims=True)
        cmax = _cummax_rows(bb, d == 1)
        cols = _rows_to_cols(bb, eye)
        for j in range(n):
            bcol_s[hh * 2 + d, r0 + j] = jnp.broadcast_to(cols[:, j:j + 1], (L, ML_HD))
            per_slot[hh, d, r0 + j] = (a[j:j + 1], cmax[j:j + 1], f_tot[j:j + 1], b_max[j:j + 1])

    def stabilisers(hh):
        cols_h = slice(hh * ML_HD, (hh + 1) * ML_HD)
        for j in range(nchunk + 1):
            vat_s[hh, j, 0:ML_HD, :] = vt_ref[0, j, cols_h, :]
            vat_s[hh, j, ML_HD:VA_ROWS, :] = ones_row
        for d in range(2):
            m = zero11
            order = [nchunk] + (list(range(nchunk)) if d == 0 else list(range(nchunk - 1, -1, -1)))
            for slot in order:
                a_row, cmax_row, f_tot, b_max = per_slot[hh, d, slot]
                m_new = jnp.maximum(f_tot + m, f_tot + b_max)
                gates[hh, d, slot] = (a_row, cmax_row, f_tot, m, m_new)
                m = m_new

    def prep_pieces(hh):
        P = functools.partial
        pieces = [P(gate_algebra, hh, 0, gl_ref, 0), P(park, hh, xc_ref), P(conv_chunk, hh, 0, nchunk, False),
                  P(gate_algebra, hh, 0, gc_ref, nchunk), P(park, hh, xl_ref)]
        extra = [P(gate_algebra, hh, 1, gl_ref, 0), P(gate_algebra, hh, 1, gc_ref, nchunk)]
        for c in range(nchunk):
            pieces.append(P(conv_chunk, hh, c, c, True))
            if c % 3 == 0 and extra:
                pieces.append(extra.pop(0))
        return pieces + extra + [P(stabilisers, hh)]

    def scores(item):
        hh, _, c = item
        return jnp.dot(k_s[hh, c * L:(c + 1) * L, :], qt_s[hh, c], preferred_element_type=F32)

    def decayed(item, st):
        hh, d, c = item
        _, cmax_row, _, m_old, _ = gates[item]
        mm = jnp.maximum(cmax_row, m_old)
        bcol = bcol_s[hh * 2 + d, c]
        logd = jnp.where(visible[d], jnp.concatenate([bcol] * (L // ML_HD), axis=1) - mm, -jnp.inf)
        return (st * jnp.exp2(logd)).astype(BF16)

    def increment(item):
        hh, d, slot = item
        _, _, f_tot, _, m_new = gates[item]
        kc = k_s[hh, slot * L:(slot + 1) * L, :]
        kw = kc * jnp.exp2(bcol_s[hh * 2 + d, slot] + (f_tot - m_new)).astype(BF16)
        upd_s[hh * 2 + d, slot] = jnp.dot(vat_s[hh, slot], kw, preferred_element_type=F32)

    def intra_steps(hh):
        latent = [(hh, d, c) for d in range(2) for c in range(nchunk)]
        every = [(hh, d, nchunk) for d in range(2)] + latent
        n_lat = len(latent)
        s_val, p_val = {}, {}

        def step(t):
            if t < n_lat:
                s_val[t] = scores(latent[t])
            if 0 <= t - 1 < n_lat:
                p_val[t - 1] = decayed(latent[t - 1], s_val.pop(t - 1))
            if 0 <= t - 2 < n_lat:
                _, d, c = latent[t - 2]
                r1_s[hh * 2 + d, c] = jnp.dot(vat_s[hh, c], p_val.pop(t - 2), preferred_element_type=F32)
            if t < len(every):
                increment(every[t])

        return [functools.partial(step, t) for t in range(max(n_lat + 2, len(every)))]

    def recurrence(hh):
        for d in range(2):
            sd = hh * 2 + d
            st = upd_s[sd, nchunk]
            for c in (range(nchunk) if d == 0 else range(nchunk - 1, -1, -1)):
                _, _, f_tot, m_old, m_new = gates[hh, d, c]
                stb_s[sd, c] = st.astype(BF16)
                st = jnp.exp2(f_tot + m_old - m_new) * st + upd_s[sd, c]

    def carried(item):
        hh, d, c = item
        return jnp.dot(stb_s[hh * 2 + d, c], qt_s[hh, c], preferred_element_type=F32)

    def emit(item, r2):
        hh, d, c = item
        a_row, cmax_row, _, m_old, _ = gates[item]
        mm = jnp.maximum(cmax_row, m_old)
        tot = r1_s[hh * 2 + d, c] + jnp.exp2(m_old - mm) * r2
        den = tot[ML_HD:ML_HD + 1, :]
        floor = jnp.exp2(-(a_row + mm))
        (hf_s if d == 0 else hb_s)[hh, c] = tot[0:ML_HD, :] * (1.0 / jnp.maximum(jnp.abs(den), floor))

    def finish(hh, c):
        cols_h = slice(hh * ML_HD, (hh + 1) * ML_HD)
        ht = hf_s[hh, c] + hb_s[hh, c]
        ms = jnp.mean(ht * ht, axis=0, keepdims=True)
        hn = (ht * lax.rsqrt(ms + EPS)).T * hg_ref[:, cols_h]
        rows = slice(c * L, (c + 1) * L)
        y = ((ob_ref[0, rows, cols_h].astype(F32) * hn + sk_ref[:, cols_h] * xcv_s[hh, rows, :])
             * zb_ref[0, rows, cols_h].astype(F32))
        o_ref[0, rows, cols_h] = y.astype(BF16)

    def output_steps(hh):
        items = [(hh, d, c) for c in range(nchunk) for d in range(2)]
        r2_val = {}

        def step(t):
            if t < len(items):
                r2_val[t] = carried(items[t])
            if t >= 1:
                emit(items[t - 1], r2_val.pop(t - 1))
                if (t - 1) % 2 == 1:
                    finish(hh, items[t - 1][2])

        return [functools.partial(step, t) for t in range(len(items) + 1)]

    def interleave(main, side):
        side = list(side)
        per = -(-len(side) // max(len(main), 1))
        for step in main:
            step()
            for _ in range(per):
                if side:
                    side.pop(0)()
        for piece in side:
            piece()

    for piece in prep_pieces(0):
        piece()
    for hh in range(ML_HPP):
        side = output_steps(hh - 1) if hh >= 1 else []
        if hh + 1 < ML_HPP:
            side = side + prep_pieces(hh + 1)
        interleave(intra_steps(hh), side)
        recurrence(hh)
    for step in output_steps(ML_HPP - 1):
        step()


def _mlstm(xm_l, xm_c, vt_all, gt_l, gt_c, zb, ob, conv_w, conv_b, wqt, wk, head_g, skip):
    b, t, _ = xm_l.shape
    n_ctx = xm_c.shape[1]
    nchunk = t // ML_L
    nslot = nchunk + 1
    assert t % (8 * ML_L) == 0 and n_ctx == ML_L and vt_all.shape[1] == nslot
    gl = gt_l.reshape(b, ML_GATES, nchunk, ML_L)
    gc = gt_c.reshape(b, ML_GATES, 1, ML_L)
    pw = ML_HPP * ML_HD
    vt_blk0 = DA_WIDTH // pw
    tokmap = lambda bi, h: (bi, 0, h)
    gmap = lambda bi, h: (bi, 0, 0, 0)
    hvec = lambda bi, h: (0, h)
    wmap = lambda bi, h: (h, 0, 0)
    return pl.pallas_call(
        _mlstm_kernel,
        grid=(b, ML_HEADS // ML_HPP),
        in_specs=[pl.BlockSpec((1, t, pw), tokmap),
                  pl.BlockSpec((1, n_ctx, pw), tokmap),
                  pl.BlockSpec((1, nslot, pw, ML_L), lambda bi, h: (bi, 0, vt_blk0 + h, 0)),
                  pl.BlockSpec((1, ML_GATES, nchunk, ML_L), gmap),
                  pl.BlockSpec((1, ML_GATES, 1, ML_L), gmap),
                  pl.BlockSpec((1, t, pw), tokmap),
                  pl.BlockSpec((1, t, pw), tokmap),
                  pl.BlockSpec((conv_w.shape[0], pw), hvec),
                  pl.BlockSpec((1, pw), hvec),
                  pl.BlockSpec((ML_HPP, ML_HD, ML_HD), wmap),
                  pl.BlockSpec((ML_HPP, ML_HD, ML_HD), wmap),
                  pl.BlockSpec((1, pw), hvec),
                  pl.BlockSpec((1, pw), hvec)],
        out_specs=pl.BlockSpec((1, t, pw), tokmap),
        out_shape=jax.ShapeDtypeStruct((b, t, ML_WIDTH), BF16),
        scratch_shapes=[pltpu.VMEM((ML_HPP, t, ML_HD), F32),
                        pltpu.VMEM((ML_HPP, nslot, ML_HD, ML_L), BF16),
                        pltpu.VMEM((ML_HPP, nslot * ML_L, ML_HD), BF16),
                        pltpu.VMEM((ML_HPP, nslot, VA_ROWS, ML_L), BF16),
                        pltpu.VMEM((ML_HPP, nchunk, ML_HD, ML_L), F32),
                        pltpu.VMEM((ML_HPP, nchunk, ML_HD, ML_L), F32),
                        pltpu.VMEM((ML_HPP * 2, nslot, VA_ROWS, ML_HD), F32),
                        pltpu.VMEM((ML_HPP * 2, nchunk, VA_ROWS, ML_HD), BF16),
                        pltpu.VMEM((ML_HPP * 2, nchunk, VA_ROWS, ML_L), F32),
                        pltpu.VMEM((ML_HPP * 2, nslot, ML_L, ML_HD), F32),
                        pltpu.VMEM((t + 16, ML_HD), F32)],
        compiler_params=pltpu.CompilerParams(dimension_semantics=("parallel", "arbitrary"),
                                             vmem_limit_bytes=VMEM_LIMIT),
        name="mlstm",
    )(xm_l, xm_c, vt_all, gl, gc, zb, ob, conv_w, conv_b, wqt, wk, head_g, skip)


def _outproj_kernel(x_ref, sc_ref, sh_ref, nw_ref, ya_ref, yb_ref, gt_ref, wg_ref, woa_ref, wob_ref, wo_ref,
                    o_ref):
    d = x_ref.shape[2]
    nt = (((1,), (1,)), ((), ()))
    a = jnp.dot(ya_ref[0], woa_ref[...], preferred_element_type=F32)
    xf = x_ref[0]
    ms = jnp.mean(xf * xf, axis=-1, keepdims=True)
    yn = xf * lax.rsqrt(ms + EPS) * nw_ref[...]
    h = (yn * (1.0 + sc_ref[0]) + sh_ref[0]).astype(BF16)
    y = _sigmoid(lax.dot_general(h, wg_ref[0:d, :], nt, preferred_element_type=F32)) * a
    b = jnp.dot(yb_ref[0], wob_ref[...], preferred_element_type=F32)
    y = y + _sigmoid(lax.dot_general(h, wg_ref[d:2 * d, :], nt, preferred_element_type=F32)) * b
    o = jnp.dot(y.astype(BF16), wo_ref[...], preferred_element_type=F32)
    o_ref[0] = xf + gt_ref[0] * o


def _out_proj(x, sc, sh, norm_w, ya, yb, gt, wg, woa, wob, wo):
    b, t, d = x.shape
    tm = min(OUT_TILE, t)
    tok = lambda width: pl.BlockSpec((1, tm, width), lambda bi, i: (bi, i, 0))
    const2 = lambda bi, i: (0, 0)
    per_batch = pl.BlockSpec((1, 1, d), lambda bi, i: (bi, 0, 0))
    return pl.pallas_call(
        _outproj_kernel,
        grid=(b, t // tm),
        in_specs=[tok(d), per_batch, per_batch, pl.BlockSpec((1, d), const2),
                  tok(DA_WIDTH), tok(ML_WIDTH), per_batch,
                  pl.BlockSpec(wg.shape, const2),
                  pl.BlockSpec(woa.shape, const2), pl.BlockSpec(wob.shape, const2),
                  pl.BlockSpec(wo.shape, const2)],
        out_specs=tok(d),
        out_shape=jax.ShapeDtypeStruct((b, t, d), x.dtype),
        compiler_params=pltpu.CompilerParams(dimension_semantics=("parallel", "parallel"),
                                             vmem_limit_bytes=VMEM_LIMIT),
        name="out_proj",
    )(x, sc, sh, norm_w, ya, yb, gt, wg, woa, wob, wo)


def _rope_tables(n_tokens):
    rows = n_tokens // GRID_W
    row_id = jnp.repeat(jnp.arange(rows, dtype=F32), GRID_W)
    col_id = jnp.tile(jnp.arange(GRID_W, dtype=F32), rows)
    n_freq = DA_HD // 4
    inv_freq = ROPE_THETA ** (-jnp.arange(n_freq, dtype=F32) / n_freq)
    ang_r = row_id[:, None] * inv_freq
    ang_c = col_id[:, None] * inv_freq
    cos = jnp.concatenate([jnp.cos(ang_r)] * 2 + [jnp.cos(ang_c)] * 2, axis=-1)
    sin = jnp.concatenate([-jnp.sin(ang_r), jnp.sin(ang_r), -jnp.sin(ang_c), jnp.sin(ang_c)], axis=-1)
    return jnp.tile(cos, (1, LANES // DA_HD)), jnp.tile(sin, (1, LANES // DA_HD))


def kernel(x, c, ctx, c_ctx, norm_w, w_mod, b_mod, w_in, b_if, da_q_norm, da_k_norm, da_lambda_q1, da_lambda_k1, da_lambda_q2, da_lambda_k2, da_head_norm, w_out_a, ml_conv_w, ml_conv_b, ml_wq, ml_wk, ml_head_norm, ml_skip, w_out_b, w_o):
    assert w_mod.shape[0] == 1, "single-layer block"
    b, t, d = x.shape

    rows = -(-(b + 1) // 8) * 8
    cc = jnp.zeros((rows, d), F32).at[:b].set(c).at[b].set(c_ctx)
    mod, lam, bound = _modulation(cc, w_mod[0], b_mod, da_lambda_q1, da_lambda_k1, da_lambda_q2, da_lambda_k2,
                                  da_q_norm, da_k_norm)
    sh_l, sc_l, gt_l = (mod[:b, i * d:(i + 1) * d].reshape(b, 1, d) for i in range(3))
    sh_c, sc_c = (mod[b:b + 1, i * d:(i + 1) * d].reshape(1, 1, d) for i in range(2))

    w_t = jnp.swapaxes(w_in[0], 0, 1).astype(BF16)
    bif = b_if.reshape(ML_GATES, 1)

    grp = np.arange(2 * LANES) // DA_HD
    bd = jnp.asarray((grp[:, None] == grp[None, :]).astype(np.float32) / DA_HD, dtype=BF16)
    gq = jnp.tile(da_q_norm, (1, DA_WIDTH // DA_HD))
    gk = jnp.tile(da_k_norm, (1, DA_WIDTH // DA_HD))
    cos, sin = _rope_tables(t)

    n_keys = t + ctx.shape[1]
    qa, k_all, vt_all, za, xm_l, zb, ob, gtl = _in_proj(
        x, sc_l, sh_l, norm_w, w_t, bif, (bd, gq, gk, cos, sin), latent=True, n_keys=n_keys, key_off=0)
    k_all, vt_all, xm_c, gtc = _in_proj(
        ctx, sc_c, sh_c, norm_w, w_t, bif, (bd, gk, k_all, vt_all), latent=False, n_keys=n_keys, key_off=t)

    ya = _diff_attn(bound[0, :1], qa, k_all, vt_all, za, lam, da_head_norm)
    yb = _mlstm(xm_l, xm_c, vt_all, gtl, gtc, zb, ob, ml_conv_w[0], ml_conv_b,
                jnp.swapaxes(ml_wq[0], 1, 2).astype(BF16), ml_wk[0].astype(BF16), ml_head_norm, ml_skip)
    return _out_proj(x, sc_l, sh_l, norm_w, ya, yb, gt_l, w_t[COL_GG:], w_out_a[0].astype(BF16), w_out_b[0].astype(BF16),
                     w_o[0].astype(BF16))
```

```python
import functools

import jax
import jax.numpy as jnp
import numpy as np
from jax import lax
from jax.experimental import pallas as pl
from jax.experimental.pallas import tpu as pltpu

F32 = jnp.float32
BF16 = jnp.bfloat16

EPS = 1e-6
ROPE_THETA = 10000.0
GRID_W = 64
LAM_INIT = 0.8 - 0.6 * 1.0

DA_HEADS = 4
DA_HD = 64
DA_HEAD_W = 2 * DA_HD
DA_WIDTH = DA_HEADS * DA_HEAD_W

ML_HEADS = 4
ML_HD = 128
ML_WIDTH = ML_HEADS * ML_HD
ML_GATES = 4 * ML_HEADS

COL_QA, COL_KA, COL_VA, COL_ZA = (i * DA_WIDTH for i in range(4))
COL_XM, COL_VM, COL_ZB, COL_OB = (4 * DA_WIDTH + i * ML_WIDTH for i in range(4))
N_MAIN = 4 * DA_WIDTH + 4 * ML_WIDTH
COL_GG = N_MAIN + ML_GATES
ML_L = 256
VA_ROWS = ML_HD + 16
LOG2E = 1.4426950408889634
ML_HPP = 2

LANES = 128
VMEM_LIMIT = 56 * 1024 * 1024

IN_TILE = 1024
IN_SLAB = 512
OUT_TILE = 1024
ATT_BLK = 256
ATT_KC = 256
ATT_UNROLL = 3
ATT_HPP = 2
ATT_EXP2_HEADROOM = 30.0
SCORE_BOUND_MARGIN = 1.02
Q_SCALE = DA_HD ** -0.5 * LOG2E


def _sigmoid(x):
    return 0.5 * jnp.tanh(0.5 * x) + 0.5


def _silu(x):
    h = 0.5 * x
    return h * jnp.tanh(h) + h


def _split3(x):
    hi = x.astype(BF16)
    r1 = x - hi.astype(F32)
    mid = r1.astype(BF16)
    lo = (r1 - mid.astype(F32)).astype(BF16)
    return hi, mid, lo


def _mod_kernel(cc_ref, w_ref, b_ref, lq1_ref, lk1_ref, lq2_ref, lk2_ref, gq_ref, gk_ref,
                mod_ref, lam_ref, bound_ref):
    a = _silu(cc_ref[...]).astype(BF16)
    mod_ref[...] = jnp.dot(a, w_ref[...].astype(BF16), preferred_element_type=F32) + b_ref[...]
    s1 = jnp.sum(lq1_ref[...] * lk1_ref[...], axis=-1, keepdims=True)
    s2 = jnp.sum(lq2_ref[...] * lk2_ref[...], axis=-1, keepdims=True)
    lam = jnp.exp(s1) - jnp.exp(s2) + LAM_INIT
    lam_ref[...] = jnp.broadcast_to(lam, lam_ref.shape)
    bq = jnp.max(jnp.abs(gq_ref[...]), axis=-1, keepdims=True)
    bk = jnp.max(jnp.abs(gk_ref[...]), axis=-1, keepdims=True)
    bound_ref[...] = jnp.broadcast_to(SCORE_BOUND_MARGIN * Q_SCALE * DA_HD * bq * bk, bound_ref.shape)


def _modulation(cc, w_mod, b_mod, lq1, lk1, lq2, lk2, gq, gk):
    rows, d = cc.shape
    n3 = w_mod.shape[1]
    vec = pl.BlockSpec((1, DA_HD), lambda j: (0, 0))
    return pl.pallas_call(
        _mod_kernel,
        grid=(n3 // d,),
        in_specs=[pl.BlockSpec((rows, d), lambda j: (0, 0)),
                  pl.BlockSpec((d, d), lambda j: (0, j)),
                  pl.BlockSpec((1, d), lambda j: (0, j)),
                  vec, vec, vec, vec, vec, vec],
        out_specs=[pl.BlockSpec((rows, d), lambda j: (0, j)),
                   pl.BlockSpec((1, LANES), lambda j: (0, 0)),
                   pl.BlockSpec((1, LANES), lambda j: (0, 0))],
        out_shape=[jax.ShapeDtypeStruct((rows, n3), F32),
                   jax.ShapeDtypeStruct((1, LANES), F32),
                   jax.ShapeDtypeStruct((1, LANES), F32)],
        compiler_params=pltpu.CompilerParams(dimension_semantics=("arbitrary",),
                                             vmem_limit_bytes=VMEM_LIMIT),
        name="modulation",
    )(cc, w_mod, b_mod, lq1, lk1, lq2, lk2, gq, gk)


def _group_rms(acc, bd_ref, g_ref):
    sq = (acc * acc).astype(BF16)
    n = bd_ref.shape[0]
    ms = jnp.concatenate([jnp.dot(sq[:, c:c + n], bd_ref[...], preferred_element_type=F32)
                          for c in range(0, acc.shape[1], n)], axis=1)
    return acc * lax.rsqrt(ms + EPS) * g_ref[...]


def _rope(x, cos, sin_signed, first_half):
    outs = []
    for c in range(x.shape[1] // LANES):
        xs = x[:, c * LANES:(c + 1) * LANES]
        partner = jnp.where(first_half, pltpu.roll(xs, LANES - 16, 1), pltpu.roll(xs, 16, 1))
        outs.append(xs * cos + partner * sin_signed)
    return jnp.concatenate(outs, axis=1)


def _inproj_kernel(x_ref, sc_ref, sh_ref, nw_ref, w_ref, bif_ref, *rest, latent):
    if latent:
        (bd_ref, gq_ref, gk_ref, cos_ref, sin_ref,
         qa_ref, ka_ref, vt_ref, za_ref, xm_ref, zb_ref, ob_ref, gt_ref) = rest
    else:
        bd_ref, gk_ref, _, _, ka_ref, vt_ref, xm_ref, gt_ref = rest

    nt = (((1,), (1,)), ((), ()))
    slab = min(IN_SLAB, x_ref.shape[1])
    nslab = x_ref.shape[1] // slab
    cps = slab // ATT_KC

    def normed(c):
        xf = x_ref[0, c * ATT_KC:(c + 1) * ATT_KC, :]
        ms = jnp.mean(xf * xf, axis=-1, keepdims=True)
        y = xf * lax.rsqrt(ms + EPS) * nw_ref[...]
        return (y * (1.0 + sc_ref[0]) + sh_ref[0]).astype(BF16)

    def values(c, hc):
        for r0, c0 in ((0, COL_VA), (DA_WIDTH, COL_VM)):
            vt_ref[0, c, r0:r0 + DA_WIDTH, :] = lax.dot_general(
                w_ref[c0:c0 + DA_WIDTH, :], hc, nt, preferred_element_type=F32).astype(BF16)

    def projections(j, h):
        rows = slice(j * slab, (j + 1) * slab)

        def proj(c0, width):
            return lax.dot_general(h, w_ref[c0:c0 + width, :], nt, preferred_element_type=F32)

        if latent:
            lane = lax.broadcasted_iota(jnp.int32, (slab, LANES), 1)
            first_half = (lane % 32) < 16
            cos = cos_ref[rows, :]
            sin = sin_ref[rows, :]
            q = _rope(_group_rms(proj(COL_QA, DA_WIDTH), bd_ref, gq_ref), cos, sin, first_half)
            qa_ref[0, rows, :] = (q * Q_SCALE).astype(BF16)
            k = _rope(_group_rms(proj(COL_KA, DA_WIDTH), bd_ref, gk_ref), cos, sin, first_half)
            ka_ref[0, rows, :] = k.astype(BF16)
            za_ref[0, rows, :] = _silu(proj(COL_ZA, DA_WIDTH)).astype(BF16)
            zb_ref[0, rows, :] = _silu(proj(COL_ZB, ML_WIDTH)).astype(BF16)
            ob_ref[0, rows, :] = _sigmoid(proj(COL_OB, ML_WIDTH)).astype(BF16)
        else:
            ka_ref[0, rows, :] = _group_rms(proj(COL_KA, DA_WIDTH), bd_ref, gk_ref).astype(BF16)

        g = proj(N_MAIN, LANES).T[:ML_GATES] + bif_ref[...]
        row = lax.broadcasted_iota(jnp.int32, g.shape, 0)
        logsig = jnp.minimum(g, 0.0) - jnp.log(1.0 + jnp.exp(-jnp.abs(g)))
        gt_ref[0, :, rows] = jnp.where((row // ML_HEADS) % 2 == 1, logsig, g)

        xm_ref[0, rows, :] = proj(COL_XM, ML_WIDTH).astype(BF16)

    for j in range(nslab):
        pieces = []
        for c in range(j * cps, (j + 1) * cps):
            pieces.append(normed(c))
            values(c, pieces[-1])
        projections(j, pieces[0] if cps == 1 else jnp.concatenate(pieces, axis=0))


def _in_proj(x, sc, sh, norm_w, w_t, bif, extra, *, latent, n_keys, key_off):
    b, t, d = x.shape
    tm = min(IN_TILE, t)
    assert key_off % tm == 0
    kblk = key_off // tm
    grid = (b, t // tm)
    const2 = lambda bi, i: (0, 0)
    mod_map = (lambda bi, i: (bi, 0, 0)) if latent else (lambda bi, i: (0, 0, 0))
    tok = lambda width, dtype: (pl.BlockSpec((1, tm, width), lambda bi, i: (bi, i, 0)),
                                jax.ShapeDtypeStruct((b, t, width), dtype))
    keys = (pl.BlockSpec((1, tm, DA_WIDTH), lambda bi, i: (bi, kblk + i, 0)),
            jax.ShapeDtypeStruct((b, n_keys, DA_WIDTH), BF16))
    cpt = tm // ATT_KC
    n_val = DA_WIDTH + ML_WIDTH
    vals = (pl.BlockSpec((1, cpt, n_val, ATT_KC), lambda bi, i: (bi, kblk + i, 0, 0)),
            jax.ShapeDtypeStruct((b, n_keys // ATT_KC, n_val, ATT_KC), BF16))
    in_specs = [pl.BlockSpec((1, tm, d), lambda bi, i: (bi, i, 0)),
                pl.BlockSpec((1, 1, d), mod_map),
                pl.BlockSpec((1, 1, d), mod_map),
                pl.BlockSpec((1, d), const2),
                pl.BlockSpec(w_t.shape, const2),
                pl.BlockSpec(bif.shape, const2)]
    if latent:
        bd, gq, gk, cos, sin = extra
        in_specs += [pl.BlockSpec(bd.shape, const2), pl.BlockSpec(gq.shape, const2),
                     pl.BlockSpec(gk.shape, const2),
                     pl.BlockSpec((tm, LANES), lambda bi, i: (i, 0)),
                     pl.BlockSpec((tm, LANES), lambda bi, i: (i, 0))]
        outs = [tok(DA_WIDTH, BF16), keys, vals, tok(DA_WIDTH, BF16)] + [tok(ML_WIDTH, BF16)] * 3
        aliases = {}
    else:
        bd, gk, _, _ = extra
        in_specs += [pl.BlockSpec(bd.shape, const2), pl.BlockSpec(gk.shape, const2),
                     pl.BlockSpec(memory_space=pl.ANY), pl.BlockSpec(memory_space=pl.ANY)]
        outs = [keys, vals, tok(ML_WIDTH, BF16)]
        aliases = {len(in_specs) - 2: 0, len(in_specs) - 1: 1}
    outs.append((pl.BlockSpec((1, ML_GATES, tm), lambda bi, i: (bi, 0, i)),
                 jax.ShapeDtypeStruct((b, ML_GATES, t), F32)))
    return pl.pallas_call(
        functools.partial(_inproj_kernel, latent=latent),
        grid=grid,
        in_specs=in_specs,
        out_specs=[o[0] for o in outs],
        out_shape=[o[1] for o in outs],
        input_output_aliases=aliases,
        compiler_params=pltpu.CompilerParams(dimension_semantics=("parallel", "parallel"),
                                             vmem_limit_bytes=VMEM_LIMIT),
        name="in_proj_latent" if latent else "in_proj_context",
    )(x, sc, sh, norm_w, w_t, bif, *extra)


def _attn_kernel(bound_ref, q_ref, k_ref, vt_ref, z_ref, lam_ref, hg_ref, o_ref, s_scr, e_scr, acc_scr, qm_scr):
    nblk = q_ref.shape[1] // ATT_BLK
    nkc = k_ref.shape[1] // ATT_KC
    sub = ATT_KC // 8
    items = [(hh, j) for hh in range(ATT_HPP) for j in range(nblk)]

    def head_cols(hh):
        return slice(hh * DA_HEAD_W, (hh + 1) * DA_HEAD_W)
    lam = lam_ref[:, 0:1]
    nt = (((1,), (1,)), ((), ()))
    neg = jnp.full((8, ATT_BLK), -jnp.inf, F32)
    zero8 = jnp.zeros((8, ATT_BLK), F32)
    bound = bound_ref[0]

    def mask_queries(item):
        hh, j = item
        q = q_ref[0, j * ATT_BLK:(j + 1) * ATT_BLK, head_cols(hh)]
        lane = lax.broadcasted_iota(jnp.int32, q.shape, 1)
        zero = jnp.zeros_like(q)
        qm_scr[0] = jnp.where(lane < DA_HD, q, zero)
        qm_scr[1] = jnp.where(lane >= DA_HD, q, zero)

    def scores(item, c, mp):
        kc = k_ref[0, pl.ds(pl.multiple_of(c * ATT_KC, ATT_KC), ATT_KC), head_cols(item[0])]
        return lax.dot_general(kc, qm_scr[mp], nt, preferred_element_type=F32).reshape(sub, 8, ATT_BLK)

    def exps(c, par, mp, st, stab8, lacc):
        e = jnp.exp2(st - stab8[None])
        e_scr[par, mp, c] = e.reshape(ATT_KC, ATT_BLK)
        return lacc + jnp.sum(e, axis=0)

    def values(item, c, par, r8):
        e1 = e_scr[par, 0, c].reshape(sub, 8, ATT_BLK)
        e2 = e_scr[par, 1, c].reshape(sub, 8, ATT_BLK)
        pt = (e1 - e2 * r8[None]).reshape(ATT_KC, ATT_BLK).astype(BF16)
        acc_scr[...] += jnp.dot(vt_ref[0, c, head_cols(item[0]), :], pt, preferred_element_type=F32)

    def finish(item, inv_l1):
        hh, j = item
        o = (acc_scr[...] * inv_l1).T
        ms = jnp.mean(o * o, axis=-1, keepdims=True)
        on = o * lax.rsqrt(ms + EPS) * hg_ref[...]
        rows = slice(j * ATT_BLK, (j + 1) * ATT_BLK)
        gate = z_ref[0, rows, head_cols(hh)].astype(F32)
        o_ref[0, rows, head_cols(hh)] = (on * (1.0 - LAM_INIT) * gate).astype(BF16)

    def pipeline(exact_max):
        d_exp = 1 if exact_max else 0
        d_val = d_exp + 1
        bound8 = jnp.full((8, ATT_BLK), bound, F32)
        m8, r8, inv_l1 = {}, {}, {}
        nit = len(items)
        for j in range(nit + d_val):
            do_s, do_e, do_v = j < nit, d_exp <= j < nit + d_exp, d_val <= j
            if do_s:
                mask_queries(items[j])
            if do_v:
                acc_scr[...] = jnp.zeros_like(acc_scr)

            def body(c, carry, j=j, do_s=do_s, do_e=do_e, do_v=do_v):
                macc, lacc = carry
                if exact_max:
                    if do_s:
                        macc = list(macc)
                        for mp in range(2):
                            st = scores(items[j], c, mp)
                            s_scr[j % 2, mp, c] = st.reshape(ATT_KC, ATT_BLK)
                            macc[mp] = jnp.maximum(macc[mp], jnp.max(st, axis=0))
                        macc = tuple(macc)
                    if do_e:
                        jb = j - d_exp
                        lacc = tuple(
                            exps(c, jb % 2, mp, s_scr[jb % 2, mp, c].reshape(sub, 8, ATT_BLK), m8[jb][mp], lacc[mp])
                            for mp in range(2))
                elif do_s:
                    lacc = tuple(exps(c, j % 2, mp, scores(items[j], c, mp), bound8, lacc[mp]) for mp in range(2))
                if do_v:
                    values(items[j - d_val], c, (j - d_val) % 2, r8[j - d_val])
                return macc, lacc

            macc, lacc = lax.fori_loop(0, nkc, body, ((neg, neg), (zero8, zero8)),
                                       unroll=ATT_UNROLL if exact_max else nkc)
            if do_v:
                finish(items[j - d_val], inv_l1.pop(j - d_val))
                r8.pop(j - d_val)
            if do_e:
                l1 = jnp.sum(lacc[0], axis=0, keepdims=True)
                l2 = jnp.sum(lacc[1], axis=0, keepdims=True)
                r8[j - d_exp] = jnp.broadcast_to(lam * l1 / l2, (8, ATT_BLK))
                inv_l1[j - d_exp] = 1.0 / l1
                m8.pop(j - d_exp, None)
            if exact_max and do_s:
                m8[j] = tuple(jnp.broadcast_to(jnp.max(a, axis=0, keepdims=True), (8, ATT_BLK)) for a in macc)

    use_bound = bound <= 0.5 * (126.0 - ATT_EXP2_HEADROOM)

    @pl.when(use_bound)
    def _():
        pipeline(False)

    @pl.when(jnp.logical_not(use_bound))
    def _():
        pipeline(True)


def _diff_attn(bound, qa, k_all, vt_all, za, lam, head_g):
    b, t, _ = qa.shape
    n_keys = k_all.shape[1]
    nkc = n_keys // ATT_KC
    assert t % ATT_BLK == 0 and n_keys % ATT_KC == 0 and nkc % ATT_UNROLL == 0
    pw = ATT_HPP * DA_HEAD_W
    qmap = lambda bi, h: (bi, 0, h)
    const2 = lambda bi, h: (0, 0)
    return pl.pallas_call(
        _attn_kernel,
        grid=(b, DA_HEADS // ATT_HPP),
        in_specs=[pl.BlockSpec(memory_space=pltpu.SMEM),
                  pl.BlockSpec((1, t, pw), qmap),
                  pl.BlockSpec((1, n_keys, pw), qmap),
                  pl.BlockSpec((1, nkc, pw, ATT_KC), lambda bi, h: (bi, 0, h, 0)),
                  pl.BlockSpec((1, t, pw), qmap),
                  pl.BlockSpec((1, LANES), const2),
                  pl.BlockSpec((1, DA_HEAD_W), const2)],
        out_specs=pl.BlockSpec((1, t, pw), qmap),
        out_shape=jax.ShapeDtypeStruct((b, t, DA_WIDTH), BF16),
        scratch_shapes=[pltpu.VMEM((2, 2, nkc, ATT_KC, ATT_BLK), F32),
                        pltpu.VMEM((2, 2, nkc, ATT_KC, ATT_BLK), F32),
                        pltpu.VMEM((DA_HEAD_W, ATT_BLK), F32),
                        pltpu.VMEM((2, ATT_BLK, DA_HEAD_W), BF16)],
        compiler_params=pltpu.CompilerParams(dimension_semantics=("parallel", "arbitrary"),
                                             vmem_limit_bytes=VMEM_LIMIT),
        name="diff_attn",
    )(bound, qa, k_all, vt_all, za, lam, head_g)


def _rows_to_cols(rows, eye):
    nt = (((1,), (1,)), ((), ()))
    out = None
    for part in _split3(rows):
        term = lax.dot_general(eye, part, nt, preferred_element_type=F32)
        out = term if out is None else out + term
    return out


def _cumsum_rows(rows, tri):
    out = None
    for part in _split3(rows):
        term = jnp.dot(part, tri, preferred_element_type=F32)
        out = term if out is None else out + term
    return out


def _cummax_rows(rows, reverse):
    n = rows.shape[1]
    lane = lax.broadcasted_iota(jnp.int32, rows.shape, 1)
    sh = 1
    while sh < n:
        if reverse:
            shifted = jnp.where(lane < n - sh, pltpu.roll(rows, n - sh, 1), -jnp.inf)
        else:
            shifted = jnp.where(lane >= sh, pltpu.roll(rows, sh, 1), -jnp.inf)
        rows = jnp.maximum(rows, shifted)
        sh *= 2
    return rows


def _mlstm_kernel(xl_ref, xc_ref, vt_ref, gl_ref, gc_ref, zb_ref, ob_ref,
                  cw_ref, cb_ref, wqt_ref, wk_ref, hg_ref, sk_ref, o_ref,
                  xcv_s, qt_s, k_s, vat_s, hf_s, hb_s, upd_s, stb_s, r1_s, bcol_s, pad_s):
    hp = pl.program_id(1)
    n_lat = xl_ref.shape[1]
    nchunk = n_lat // ML_L
    L = ML_L
    nt = (((1,), (1,)), ((), ()))

    r_i = lax.broadcasted_iota(jnp.int32, (L, L), 0)
    c_i = lax.broadcasted_iota(jnp.int32, (L, L), 1)
    eye = jnp.where(r_i == c_i, 1.0, 0.0).astype(BF16)
    tris = (jnp.where(r_i <= c_i, 1.0, 0.0).astype(BF16), jnp.where(r_i >= c_i, 1.0, 0.0).astype(BF16))
    visible = (r_i <= c_i, r_i >= c_i)
    row = lax.broadcasted_iota(jnp.int32, (VA_ROWS - ML_HD, L), 0)
    ones_row = jnp.where(row == 0, 1.0, 0.0).astype(BF16)
    lane8 = lax.broadcasted_iota(jnp.int32, (8, L), 1)
    zero11 = jnp.zeros((1, 1), F32)

    gates = {}
    per_slot = {}

    def park(hh, x_ref):
        n = x_ref.shape[1]
        edge = jnp.zeros((8, ML_HD), F32)
        pad_s[0:8, :] = edge
        pad_s[8:8 + n, :] = x_ref[0, :, hh * ML_HD:(hh + 1) * ML_HD].astype(F32)
        pad_s[8 + n:16 + n, :] = edge

    def conv_chunk(hh, c, slot, keep):
        cols_h = slice(hh * ML_HD, (hh + 1) * ML_HD)
        w = cw_ref[:, cols_h]
        y = cb_ref[:, cols_h] + w[0:1, :] * pad_s[7 + c * L:7 + (c + 1) * L, :]
        y = y + w[1:2, :] * pad_s[8 + c * L:8 + (c + 1) * L, :]
        y = y + w[2:3, :] * pad_s[9 + c * L:9 + (c + 1) * L, :]
        xc = _silu(y)
        if keep:
            xcv_s[hh, c * L:(c + 1) * L, :] = xc
        xb = xc.astype(BF16)
        qt_s[hh, slot] = lax.dot_general(wqt_ref[hh], xb, nt, preferred_element_type=F32).astype(BF16)
        k = jnp.dot(xb, wk_ref[hh], preferred_element_type=F32) * (ML_HD ** -0.5)
        k_s[hh, slot * L:(slot + 1) * L, :] = k.astype(BF16)

    def gate_algebra(hh, d, g_ref, r0):
        head = hp * ML_HPP + hh
        ig = g_ref[0, (2 * d) * ML_HEADS + head] * LOG2E
        lf = g_ref[0, (2 * d + 1) * ML_HEADS + head] * LOG2E
        n = ig.shape[0]
        if n == 1:
            ig = jnp.broadcast_to(ig, (8, L))
            lf = jnp.broadcast_to(lf, (8, L))
        a = _cumsum_rows(lf, tris[d])
        bb = ig - a
        pick = (lane8 == L - 1) if d == 0 else (lane8 == 0)
        f_tot = jnp.sum(jnp.where(pick, a, 0.0), axis=-1, keepdims=True)
        b_max = jnp.max(bb, axis=-1, keepdims=True)
        cmax = _cummax_rows(bb, d == 1)
        cols = _rows_to_cols(bb, eye)
        for j in range(n):
            bcol_s[hh * 2 + d, r0 + j] = jnp.broadcast_to(cols[:, j:j + 1], (L, ML_HD))
            per_slot[hh, d, r0 + j] = (a[j:j + 1], cmax[j:j + 1], f_tot[j:j + 1], b_max[j:j + 1])

    def stabilisers(hh):
        cols_h = slice(hh * ML_HD, (hh + 1) * ML_HD)
        for j in range(nchunk + 1):
            vat_s[hh, j, 0:ML_HD, :] = vt_ref[0, j, cols_h, :]
            vat_s[hh, j, ML_HD:VA_ROWS, :] = ones_row
        for d in range(2):
            m = zero11
            order = [nchunk] + (list(range(nchunk)) if d == 0 else list(range(nchunk - 1, -1, -1)))
            for slot in order:
                a_row, cmax_row, f_tot, b_max = per_slot[hh, d, slot]
                m_new = jnp.maximum(f_tot + m, f_tot + b_max)
                gates[hh, d, slot] = (a_row, cmax_row, f_tot, m, m_new)
                m = m_new

    def prep_pieces(hh):
        P = functools.partial
        pieces = [P(gate_algebra, hh, 0, gl_ref, 0), P(park, hh, xc_ref), P(conv_chunk, hh, 0, nchunk, False),
                  P(gate_algebra, hh, 0, gc_ref, nchunk), P(park, hh, xl_ref)]
        extra = [P(gate_algebra, hh, 1, gl_ref, 0), P(gate_algebra, hh, 1, gc_ref, nchunk)]
        for c in range(nchunk):
            pieces.append(P(conv_chunk, hh, c, c, True))
            if c % 3 == 0 and extra:
                pieces.append(extra.pop(0))
        return pieces + extra + [P(stabilisers, hh)]

    def scores(item):
        hh, _, c = item
        return jnp.dot(k_s[hh, c * L:(c + 1) * L, :], qt_s[hh, c], preferred_element_type=F32)

    def decayed(item, st):
        hh, d, c = item
        _, cmax_row, _, m_old, _ = gates[item]
        mm = jnp.maximum(cmax_row, m_old)
        bcol = bcol_s[hh * 2 + d, c]
        logd = jnp.where(visible[d], jnp.concatenate([bcol] * (L // ML_HD), axis=1) - mm, -jnp.inf)
        return (st * jnp.exp2(logd)).astype(BF16)

    def increment(item):
        hh, d, slot = item
        _, _, f_tot, _, m_new = gates[item]
        kc = k_s[hh, slot * L:(slot + 1) * L, :]
        kw = kc * jnp.exp2(bcol_s[hh * 2 + d, slot] + (f_tot - m_new)).astype(BF16)
        upd_s[hh * 2 + d, slot] = jnp.dot(vat_s[hh, slot], kw, preferred_element_type=F32)

    def intra_steps(hh):
        latent = [(hh, d, c) for d in range(2) for c in range(nchunk)]
        every = [(hh, d, nchunk) for d in range(2)] + latent
        n_lat = len(latent)
        s_val, p_val = {}, {}

        def step(t):
            if t < n_lat:
                s_val[t] = scores(latent[t])
            if 0 <= t - 1 < n_lat:
                p_val[t - 1] = decayed(latent[t - 1], s_val.pop(t - 1))
            if 0 <= t - 2 < n_lat:
                _, d, c = latent[t - 2]
                r1_s[hh * 2 + d, c] = jnp.dot(vat_s[hh, c], p_val.pop(t - 2), preferred_element_type=F32)
            if t < len(every):
                increment(every[t])

        return [functools.partial(step, t) for t in range(max(n_lat + 2, len(every)))]

    def recurrence(hh):
        for d in range(2):
            sd = hh * 2 + d
            st = upd_s[sd, nchunk]
            for c in (range(nchunk) if d == 0 else range(nchunk - 1, -1, -1)):
                _, _, f_tot, m_old, m_new = gates[hh, d, c]
                stb_s[sd, c] = st.astype(BF16)
                st = jnp.exp2(f_tot + m_old - m_new) * st + upd_s[sd, c]

    def carried(item):
        hh, d, c = item
        return jnp.dot(stb_s[hh * 2 + d, c], qt_s[hh, c], preferred_element_type=F32)

    def emit(item, r2):
        hh, d, c = item
        a_row, cmax_row, _, m_old, _ = gates[item]
        mm = jnp.maximum(cmax_row, m_old)
        tot = r1_s[hh * 2 + d, c] + jnp.exp2(m_old - mm) * r2
        den = tot[ML_HD:ML_HD + 1, :]
        floor = jnp.exp2(-(a_row + mm))
        (hf_s if d == 0 else hb_s)[hh, c] = tot[0:ML_HD, :] * (1.0 / jnp.maximum(jnp.abs(den), floor))

    def finish(hh, c):
        cols_h = slice(hh * ML_HD, (hh + 1) * ML_HD)
        ht = hf_s[hh, c] + hb_s[hh, c]
        ms = jnp.mean(ht * ht, axis=0, keepdims=True)
        hn = (ht * lax.rsqrt(ms + EPS)).T * hg_ref[:, cols_h]
        rows = slice(c * L, (c + 1) * L)
        y = ((ob_ref[0, rows, cols_h].astype(F32) * hn + sk_ref[:, cols_h] * xcv_s[hh, rows, :])
             * zb_ref[0, rows, cols_h].astype(F32))
        o_ref[0, rows, cols_h] = y.astype(BF16)

    def output_steps(hh):
        items = [(hh, d, c) for c in range(nchunk) for d in range(2)]
        r2_val = {}

        def step(t):
            if t < len(items):
                r2_val[t] = carried(items[t])
            if t >= 1:
                emit(items[t - 1], r2_val.pop(t - 1))
                if (t - 1) % 2 == 1:
                    finish(hh, items[t - 1][2])

        return [functools.partial(step, t) for t in range(len(items) + 1)]

    def interleave(main, side):
        side = list(side)
        per = -(-len(side) // max(len(main), 1))
        for step in main:
            step()
            for _ in range(per):
                if side:
                    side.pop(0)()
        for piece in side:
            piece()

    for piece in prep_pieces(0):
        piece()
    for hh in range(ML_HPP):
        side = output_steps(hh - 1) if hh >= 1 else []
        if hh + 1 < ML_HPP:
            side = side + prep_pieces(hh + 1)
        interleave(intra_steps(hh), side)
        recurrence(hh)
    for step in output_steps(ML_HPP - 1):
        step()


def _mlstm(xm_l, xm_c, vt_all, gt_l, gt_c, zb, ob, conv_w, conv_b, wqt, wk, head_g, skip):
    b, t, _ = xm_l.shape
    n_ctx = xm_c.shape[1]
    nchunk = t // ML_L
    nslot = nchunk + 1
    assert t % (8 * ML_L) == 0 and n_ctx == ML_L and vt_all.shape[1] == nslot
    gl = gt_l.reshape(b, ML_GATES, nchunk, ML_L)
    gc = gt_c.reshape(b, ML_GATES, 1, ML_L)
    pw = ML_HPP * ML_HD
    vt_blk0 = DA_WIDTH // pw
    tokmap = lambda bi, h: (bi, 0, h)
    gmap = lambda bi, h: (bi, 0, 0, 0)
    hvec = lambda bi, h: (0, h)
    wmap = lambda bi, h: (h, 0, 0)
    return pl.pallas_call(
        _mlstm_kernel,
        grid=(b, ML_HEADS // ML_HPP),
        in_specs=[pl.BlockSpec((1, t, pw), tokmap),
                  pl.BlockSpec((1, n_ctx, pw), tokmap),
                  pl.BlockSpec((1, nslot, pw, ML_L), lambda bi, h: (bi, 0, vt_blk0 + h, 0)),
                  pl.BlockSpec((1, ML_GATES, nchunk, ML_L), gmap),
                  pl.BlockSpec((1, ML_GATES, 1, ML_L), gmap),
                  pl.BlockSpec((1, t, pw), tokmap),
                  pl.BlockSpec((1, t, pw), tokmap),
                  pl.BlockSpec((conv_w.shape[0], pw), hvec),
                  pl.BlockSpec((1, pw), hvec),
                  pl.BlockSpec((ML_HPP, ML_HD, ML_HD), wmap),
                  pl.BlockSpec((ML_HPP, ML_HD, ML_HD), wmap),
                  pl.BlockSpec((1, pw), hvec),
                  pl.BlockSpec((1, pw), hvec)],
        out_specs=pl.BlockSpec((1, t, pw), tokmap),
        out_shape=jax.ShapeDtypeStruct((b, t, ML_WIDTH), BF16),
        scratch_shapes=[pltpu.VMEM((ML_HPP, t, ML_HD), F32),
                        pltpu.VMEM((ML_HPP, nslot, ML_HD, ML_L), BF16),
                        pltpu.VMEM((ML_HPP, nslot * ML_L, ML_HD), BF16),
                        pltpu.VMEM((ML_HPP, nslot, VA_ROWS, ML_L), BF16),
                        pltpu.VMEM((ML_HPP, nchunk, ML_HD, ML_L), F32),
                        pltpu.VMEM((ML_HPP, nchunk, ML_HD, ML_L), F32),
                        pltpu.VMEM((ML_HPP * 2, nslot, VA_ROWS, ML_HD), F32),
                        pltpu.VMEM((ML_HPP * 2, nchunk, VA_ROWS, ML_HD), BF16),
                        pltpu.VMEM((ML_HPP * 2, nchunk, VA_ROWS, ML_L), F32),
                        pltpu.VMEM((ML_HPP * 2, nslot, ML_L, ML_HD), F32),
                        pltpu.VMEM((t + 16, ML_HD), F32)],
        compiler_params=pltpu.CompilerParams(dimension_semantics=("parallel", "arbitrary"),
                                             vmem_limit_bytes=VMEM_LIMIT),
        name="mlstm",
    )(xm_l, xm_c, vt_all, gl, gc, zb, ob, conv_w, conv_b, wqt, wk, head_g, skip)


def _outproj_kernel(x_ref, sc_ref, sh_ref, nw_ref, ya_ref, yb_ref, gt_ref, wg_ref, woa_ref, wob_ref, wo_ref,
                    o_ref):
    d = x_ref.shape[2]
    nt = (((1,), (1,)), ((), ()))
    a = jnp.dot(ya_ref[0], woa_ref[...], preferred_element_type=F32)
    xf = x_ref[0]
    ms = jnp.mean(xf * xf, axis=-1, keepdims=True)
    yn = xf * lax.rsqrt(ms + EPS) * nw_ref[...]
    h = (yn * (1.0 + sc_ref[0]) + sh_ref[0]).astype(BF16)
    y = _sigmoid(lax.dot_general(h, wg_ref[0:d, :], nt, preferred_element_type=F32)) * a
    b = jnp.dot(yb_ref[0], wob_ref[...], preferred_element_type=F32)
    y = y + _sigmoid(lax.dot_general(h, wg_ref[d:2 * d, :], nt, preferred_element_type=F32)) * b
    o = jnp.dot(y.astype(BF16), wo_ref[...], preferred_element_type=F32)
    o_ref[0] = xf + gt_ref[0] * o


def _out_proj(x, sc, sh, norm_w, ya, yb, gt, wg, woa, wob, wo):
    b, t, d = x.shape
    tm = min(OUT_TILE, t)
    tok = lambda width: pl.BlockSpec((1, tm, width), lambda bi, i: (bi, i, 0))
    const2 = lambda bi, i: (0, 0)
    per_batch = pl.BlockSpec((1, 1, d), lambda bi, i: (bi, 0, 0))
    return pl.pallas_call(
        _outproj_kernel,
        grid=(b, t // tm),
        in_specs=[tok(d), per_batch, per_batch, pl.BlockSpec((1, d), const2),
                  tok(DA_WIDTH), tok(ML_WIDTH), per_batch,
                  pl.BlockSpec(wg.shape, const2),
                  pl.BlockSpec(woa.shape, const2), pl.BlockSpec(wob.shape, const2),
                  pl.BlockSpec(wo.shape, const2)],
        out_specs=tok(d),
        out_shape=jax.ShapeDtypeStruct((b, t, d), x.dtype),
        compiler_params=pltpu.CompilerParams(dimension_semantics=("parallel", "parallel"),
                                             vmem_limit_bytes=VMEM_LIMIT),
        name="out_proj",
    )(x, sc, sh, norm_w, ya, yb, gt, wg, woa, wob, wo)


def _rope_tables(n_tokens):
    rows = n_tokens // GRID_W
    row_id = jnp.repeat(jnp.arange(rows, dtype=F32), GRID_W)
    col_id = jnp.tile(jnp.arange(GRID_W, dtype=F32), rows)
    n_freq = DA_HD // 4
    inv_freq = ROPE_THETA ** (-jnp.arange(n_freq, dtype=F32) / n_freq)
    ang_r = row_id[:, None] * inv_freq
    ang_c = col_id[:, None] * inv_freq
    cos = jnp.concatenate([jnp.cos(ang_r)] * 2 + [jnp.cos(ang_c)] * 2, axis=-1)
    sin = jnp.concatenate([-jnp.sin(ang_r), jnp.sin(ang_r), -jnp.sin(ang_c), jnp.sin(ang_c)], axis=-1)
    return jnp.tile(cos, (1, LANES // DA_HD)), jnp.tile(sin, (1, LANES // DA_HD))


def kernel(x, c, ctx, c_ctx, norm_w, w_mod, b_mod, w_in, b_if, da_q_norm, da_k_norm, da_lambda_q1, da_lambda_k1, da_lambda_q2, da_lambda_k2, da_head_norm, w_out_a, ml_conv_w, ml_conv_b, ml_wq, ml_wk, ml_head_norm, ml_skip, w_out_b, w_o):
    assert w_mod.shape[0] == 1, "single-layer block"
    b, t, d = x.shape

    rows = -(-(b + 1) // 8) * 8
    cc = jnp.zeros((rows, d), F32).at[:b].set(c).at[b].set(c_ctx)
    mod, lam, bound = _modulation(cc, w_mod[0], b_mod, da_lambda_q1, da_lambda_k1, da_lambda_q2, da_lambda_k2,
                                  da_q_norm, da_k_norm)
    sh_l, sc_l, gt_l = (mod[:b, i * d:(i + 1) * d].reshape(b, 1, d) for i in range(3))
    sh_c, sc_c = (mod[b:b + 1, i * d:(i + 1) * d].reshape(1, 1, d) for i in range(2))

    w_t = jnp.swapaxes(w_in[0], 0, 1).astype(BF16)
    bif = b_if.reshape(ML_GATES, 1)

    grp = np.arange(2 * LANES) // DA_HD
    bd = jnp.asarray((grp[:, None] == grp[None, :]).astype(np.float32) / DA_HD, dtype=BF16)
    gq = jnp.tile(da_q_norm, (1, DA_WIDTH // DA_HD))
    gk = jnp.tile(da_k_norm, (1, DA_WIDTH // DA_HD))
    cos, sin = _rope_tables(t)

    n_keys = t + ctx.shape[1]
    qa, k_all, vt_all, za, xm_l, zb, ob, gtl = _in_proj(
        x, sc_l, sh_l, norm_w, w_t, bif, (bd, gq, gk, cos, sin), latent=True, n_keys=n_keys, key_off=0)
    k_all, vt_all, xm_c, gtc = _in_proj(
        ctx, sc_c, sh_c, norm_w, w_t, bif, (bd, gk, k_all, vt_all), latent=False, n_keys=n_keys, key_off=t)

    ya = _diff_attn(bound[0, :1], qa, k_all, vt_all, za, lam, da_head_norm)
    yb = _mlstm(xm_l, xm_c, vt_all, gtl, gtc, zb, ob, ml_conv_w[0], ml_conv_b,
                jnp.swapaxes(ml_wq[0], 1, 2).astype(BF16), ml_wk[0].astype(BF16), ml_head_norm, ml_skip)
    return _out_proj(x, sc_l, sh_l, norm_w, ya, yb, gt_l, w_t[COL_GG:], w_out_a[0].astype(BF16), w_out_b[0].astype(BF16),
                     w_o[0].astype(BF16))
```
